```python
import jax, jax.numpy as jnp
from jax import lax
import numpy as np

D_MODEL = 2048
BATCH = 4
SEQ = 8192
DEPTH = 4

D_MIX = D_MODEL
D_HALF = D_MIX // 2
D_FF = 5632
CONV_W = 4
LRU_HEADS = 8
LRU_HDIM = D_HALF // LRU_HEADS
LRU_C = 8.0
GLA_HEADS = 4
GLA_DK = D_HALF // 2 // GLA_HEADS
GLA_DV = D_HALF // GLA_HEADS
GLA_RANK = 16
GLA_TAU = 16.0
GLA_CHUNK = 64
MLSTM_HEADS = 4
MLSTM_HDIM = D_HALF // MLSTM_HEADS
MLSTM_BLOCK = 4
MLSTM_CHUNK = 64
POOL_WINDOWS = (2, 4, 8, 16)
POOL_GROUPS = 4
POOL_GDIM = D_HALF // POOL_GROUPS
EPS = 1e-6

EVEN_SPLITS = (D_HALF, D_HALF, GLA_HEADS * GLA_DK, GLA_HEADS * GLA_DK,
               GLA_HEADS * GLA_DV, GLA_HEADS * GLA_DV, GLA_RANK)
EVEN_IN = sum(EVEN_SPLITS)
ODD_SPLITS = (D_HALF, D_HALF, D_HALF)
ODD_IN = sum(ODD_SPLITS)
N_EVEN = (DEPTH + 1) // 2
N_ODD = DEPTH // 2

kernel_name = 'hybrid_rglru_gla_mlstm_pool_macaron'


def split_cols(p, sizes):
    idx, acc = [], 0
    for s in sizes[:-1]:
        acc += s
        idx.append(acc)
    return jnp.split(p, idx, axis=-1)


def rmsnorm(x, g):
    xf = x.astype(jnp.float32)
    y = xf * lax.rsqrt(jnp.mean(xf * xf, axis=-1, keepdims=True) + EPS)
    return (y * g.astype(jnp.float32)).astype(x.dtype)


def head_rmsnorm(h, w):
    B, S, H, d = h.shape
    hf = h.astype(jnp.float32)
    hf = hf * lax.rsqrt(jnp.mean(hf * hf, axis=-1, keepdims=True) + EPS)
    return (hf.reshape(B, S, H * d) * w.astype(jnp.float32)).astype(h.dtype)


def swiglu(x, wg, wu, wd):
    return (jax.nn.silu(x @ wg) * (x @ wu)) @ wd


def causal_dwconv(x, w, b):
    C = x.shape[-1]
    y = lax.conv_general_dilated(x, w[:, None, :], window_strides=(1,),
                                 padding=[(w.shape[0] - 1, 0)],
                                 dimension_numbers=('NWC', 'WIO', 'NWC'),
                                 feature_group_count=C)
    return y + b


def blockdiag(x, w):
    B, S, _ = x.shape
    nb, bs, _ = w.shape
    return jnp.einsum('bsgi,gij->bsgj', x.reshape(B, S, nb, bs), w).reshape(B, S, nb * bs)


def rg_lru(xc, wa, ba, wx, bx, lam):
    r = jax.nn.sigmoid((blockdiag(xc, wa) + ba).astype(jnp.float32))
    i = jax.nn.sigmoid((blockdiag(xc, wx) + bx).astype(jnp.float32))
    log_a = -LRU_C * r * jax.nn.softplus(-lam.astype(jnp.float32))
    a = jnp.exp(log_a)
    b_in = jnp.sqrt(-jnp.expm1(2.0 * log_a)) * (i * xc.astype(jnp.float32))

    def combine(left, right):
        a1, b1 = left
        a2, b2 = right
        return a1 * a2, a2 * b1 + b2

    _, h = lax.associative_scan(combine, (a, b_in), axis=1)
    return h


def gla_chunked(q, k, v, log_alpha):
    B, S, H, dk = q.shape
    dv = v.shape[-1]
    L = GLA_CHUNK
    N = S // L

    def to_chunks(t):
        return t.astype(jnp.float32).reshape(B, N, L, H, t.shape[-1]).transpose(1, 0, 3, 2, 4)

    qc = to_chunks(q) * (dk ** -0.5)
    kc, vc, gc = to_chunks(k), to_chunks(v), to_chunks(log_alpha)
    causal = jnp.tril(jnp.ones((L, L), dtype=bool))

    def step(state, inp):
        qi, ki, vi, gi = inp
        bcum = jnp.cumsum(gi, axis=2)
        q_dec = qi * jnp.exp(bcum)
        k_inv = ki * jnp.exp(-bcum)
        scores = jnp.where(causal, jnp.einsum('bhtd,bhsd->bhts', q_dec, k_inv), 0.0)
        o = (jnp.einsum('bhts,bhsv->bhtv', scores, vi)
             + jnp.einsum('bhtd,bhdv->bhtv', q_dec, state))
        b_last = bcum[:, :, -1:, :]
        k_dec = ki * jnp.exp(b_last - bcum)
        new_state = (jnp.exp(b_last[:, :, 0, :])[..., None] * state
                     + jnp.einsum('bhsd,bhsv->bhdv', k_dec, vi))
        return new_state, o

    s0 = jnp.zeros((B, H, dk, dv), jnp.float32)
    _, o = lax.scan(step, s0, (qc, kc, vc, gc))
    return o.transpose(1, 0, 3, 2, 4).reshape(B, S, H, dv)


def mlstm_chunked(q, k, v, i_pre, f_pre):
    B, S, H, dh = q.shape
    L = MLSTM_CHUNK
    N = S // L

    def to_chunks(t):
        return t.astype(jnp.float32).reshape(B, N, L, H, dh).transpose(1, 0, 3, 2, 4)

    def gate_chunks(t):
        return t.astype(jnp.float32).reshape(B, N, L, H).transpose(1, 0, 3, 2)

    qc, vc = to_chunks(q), to_chunks(v)
    kc = to_chunks(k) * (dh ** -0.5)
    ic = gate_chunks(i_pre)
    lfc = jax.nn.log_sigmoid(gate_chunks(f_pre))
    causal = jnp.tril(jnp.ones((L, L), dtype=bool))

    def step(carry, inp):
        C, n, m = carry
        qi, ki, vi, ii, lfi = inp
        bcum = jnp.cumsum(lfi, axis=-1)
        D = bcum[..., :, None] - bcum[..., None, :] + ii[..., None, :]
        D = jnp.where(causal, D, -jnp.inf)
        inter = bcum + m[..., None]
        m_t = jnp.maximum(inter, jnp.max(D, axis=-1))
        W = jnp.exp(D - m_t[..., None])
        inter_w = jnp.exp(inter - m_t)
        qk = jnp.einsum('bhtd,bhsd->bhts', qi, ki) * W
        num = (jnp.einsum('bhts,bhsv->bhtv', qk, vi)
               + inter_w[..., None] * jnp.einsum('bhtd,bhvd->bhtv', qi, C))
        den = jnp.sum(qk, axis=-1) + inter_w * jnp.einsum('bhtd,bhd->bht', qi, n)
        h = num / jnp.maximum(jnp.abs(den), jnp.exp(-m_t))[..., None]
        b_L = bcum[..., -1]
        g = b_L[..., None] - bcum + ii
        m_new = jnp.maximum(b_L + m, jnp.max(g, axis=-1))
        decay = jnp.exp(b_L + m - m_new)
        w_s = jnp.exp(g - m_new[..., None])
        C_new = decay[..., None, None] * C + jnp.einsum('bhs,bhsv,bhsd->bhvd', w_s, vi, ki)
        n_new = decay[..., None] * n + jnp.einsum('bhs,bhsd->bhd', w_s, ki)
        return (C_new, n_new, m_new), h

    init = (jnp.zeros((B, H, dh, dh), jnp.float32),
            jnp.zeros((B, H, dh), jnp.float32),
            jnp.zeros((B, H), jnp.float32))
    _, h = lax.scan(step, init, (qc, kc, vc, ic, lfc))
    return h.transpose(1, 0, 3, 2, 4).reshape(B, S, H, dh)


def multiscale_pool(u, w, scale):
    B, S, _ = u.shape
    ug = u.astype(jnp.float32).reshape(B, S, POOL_GROUPS, POOL_GDIM)
    c0 = jnp.concatenate([jnp.zeros((B, 1, POOL_GROUPS, POOL_GDIM), jnp.float32),
                          jnp.cumsum(ug, axis=1)], axis=1)
    pos = jnp.arange(S)
    outs = []
    for gi, win in enumerate(POOL_WINDOWS):
        cg = c0[:, :, gi]
        lag = jnp.pad(cg, ((0, 0), (win - 1, 0), (0, 0)))[:, :S]
        cnt = jnp.minimum(pos + 1, win).astype(jnp.float32)[None, :, None]
        outs.append((cg[:, 1:] - lag) / cnt - ug[:, :, gi])
    p = jnp.stack(outs, axis=2)
    y = jnp.einsum('bsgi,gij->bsgj', p, w.astype(jnp.float32)).reshape(B, S, D_HALF)
    return (y * scale.astype(jnp.float32)).astype(u.dtype)


def even_mixer(u, w_in, w_out, conv_w, conv_b, wa, ba, wx, bx, lam, g_up, g_b, gla_norm):
    B, S, _ = u.shape
    xb, yb, q, k, v, r, glr = split_cols(u @ w_in, EVEN_SPLITS)
    xc = causal_dwconv(xb, conv_w, conv_b)
    a_out = (rg_lru(xc, wa, ba, wx, bx, lam) * jax.nn.gelu(yb.astype(jnp.float32))).astype(u.dtype)
    log_alpha = jax.nn.log_sigmoid((glr @ g_up + g_b).astype(jnp.float32)) / GLA_TAU
    o = gla_chunked(q.reshape(B, S, GLA_HEADS, GLA_DK), k.reshape(B, S, GLA_HEADS, GLA_DK),
                    v.reshape(B, S, GLA_HEADS, GLA_DV), log_alpha.reshape(B, S, GLA_HEADS, GLA_DK))
    b_out = head_rmsnorm(o.astype(u.dtype), gla_norm) * jax.nn.silu(r)
    return jnp.concatenate([a_out, b_out], axis=-1) @ w_out


def odd_mixer(u, w_in, w_out, conv_w, conv_b, wq, wk, wv, w_if, b_if, skip, mnorm, pool_w, pool_scale):
    B, S, _ = u.shape
    xm, z, pu = split_cols(u @ w_in, ODD_SPLITS)
    xc = jax.nn.silu(causal_dwconv(xm, conv_w, conv_b))
    q = blockdiag(xc, wq)
    k = blockdiag(xc, wk)
    v = blockdiag(xm, wv)
    gates = jnp.concatenate([q, k, v], axis=-1) @ w_if + b_if
    shp = (B, S, MLSTM_HEADS, MLSTM_HDIM)
    h = mlstm_chunked(q.reshape(shp), k.reshape(shp), v.reshape(shp),
                      gates[..., :MLSTM_HEADS], gates[..., MLSTM_HEADS:])
    c_out = jax.nn.sigmoid(z) * (head_rmsnorm(h.astype(u.dtype), mnorm) + skip * xc)
    d_out = multiscale_pool(pu, pool_w, pool_scale)
    return jnp.concatenate([c_out, d_out], axis=-1) @ w_out


def setup_inputs(seed: int = 0) -> dict:
    key = jax.random.key(seed)
    ks = jax.random.split(key, 30)
    f32 = jnp.float32

    def nrm(k, shape, scale):
        return jax.random.normal(k, shape, f32) * scale

    a0 = jax.random.uniform(ks[13], (N_EVEN, D_HALF), f32, 0.9, 0.999)
    p = a0 ** (1.0 / LRU_C)
    lru_lambda = jnp.log(p) - jnp.log1p(-p)
    f_bias = jnp.broadcast_to(jnp.linspace(3.0, 6.0, MLSTM_HEADS, dtype=f32), (N_ODD, MLSTM_HEADS))
    b_if = jnp.concatenate([nrm(ks[25], (N_ODD, MLSTM_HEADS), 0.1),
                            f_bias + nrm(ks[26], (N_ODD, MLSTM_HEADS), 0.1)], axis=-1)
    nb = D_HALF // MLSTM_BLOCK
    return {
        'x': nrm(ks[0], (BATCH, SEQ, D_MODEL), 1.0),
        'norm_g': 1.0 + nrm(ks[1], (DEPTH, 6, D_MODEL), 0.02),
        'ffn_wg': nrm(ks[2], (DEPTH, 2, D_MODEL, D_FF), D_MODEL ** -0.5),
        'ffn_wu': nrm(ks[3], (DEPTH, 2, D_MODEL, D_FF), D_MODEL ** -0.5),
        'ffn_wd': nrm(ks[4], (DEPTH, 2, D_FF, D_MODEL), D_FF ** -0.5),
        'ev_w_in': nrm(ks[5], (N_EVEN, D_MODEL, EVEN_IN), D_MODEL ** -0.5),
        'ev_w_out': nrm(ks[6], (N_EVEN, D_MIX, D_MODEL), D_MIX ** -0.5),
        'lru_conv_w': nrm(ks[7], (N_EVEN, CONV_W, D_HALF), CONV_W ** -0.5),
        'lru_conv_b': nrm(ks[8], (N_EVEN, D_HALF), 0.02),
        'lru_wa': nrm(ks[9], (N_EVEN, LRU_HEADS, LRU_HDIM, LRU_HDIM), LRU_HDIM ** -0.5),
        'lru_ba': nrm(ks[10], (N_EVEN, D_HALF), 0.02),
        'lru_wx': nrm(ks[11], (N_EVEN, LRU_HEADS, LRU_HDIM, LRU_HDIM), LRU_HDIM ** -0.5),
        'lru_bx': nrm(ks[12], (N_EVEN, D_HALF), 0.02),
        'lru_lambda': lru_lambda,
        'gla_w_gate': nrm(ks[14], (N_EVEN, GLA_RANK, GLA_HEADS * GLA_DK), GLA_RANK ** -0.5),
        'gla_b_gate': nrm(ks[15], (N_EVEN, GLA_HEADS * GLA_DK), 0.02),
        'gla_norm': 1.0 + nrm(ks[16], (N_EVEN, GLA_HEADS * GLA_DV), 0.02),
        'od_w_in': nrm(ks[17], (N_ODD, D_MODEL, ODD_IN), D_MODEL ** -0.5),
        'od_w_out': nrm(ks[18], (N_ODD, D_MIX, D_MODEL), D_MIX ** -0.5),
        'mlstm_conv_w': nrm(ks[19], (N_ODD, CONV_W, D_HALF), CONV_W ** -0.5),
        'mlstm_conv_b': nrm(ks[20], (N_ODD, D_HALF), 0.02),
        'mlstm_wq': nrm(ks[21], (N_ODD, nb, MLSTM_BLOCK, MLSTM_BLOCK), MLSTM_BLOCK ** -0.5),
        'mlstm_wk': nrm(ks[22], (N_ODD, nb, MLSTM_BLOCK, MLSTM_BLOCK), MLSTM_BLOCK ** -0.5),
        'mlstm_wv': nrm(ks[23], (N_ODD, nb, MLSTM_BLOCK, MLSTM_BLOCK), MLSTM_BLOCK ** -0.5),
        'mlstm_w_if': nrm(ks[24], (N_ODD, 3 * D_HALF, 2 * MLSTM_HEADS), (3 * D_HALF) ** -0.5),
        'mlstm_b_if': b_if,
        'mlstm_skip': 1.0 + nrm(ks[27], (N_ODD, D_HALF), 0.02),
        'mlstm_norm': 1.0 + nrm(ks[28], (N_ODD, D_HALF), 0.02),
        'pool_w': nrm(ks[29], (N_ODD, POOL_GROUPS, POOL_GDIM, POOL_GDIM), POOL_GDIM ** -0.5),
        'pool_scale': 1.0 + nrm(jax.random.fold_in(ks[29], 1), (N_ODD, D_HALF), 0.02),
    }


def reference(x, norm_g, ffn_wg, ffn_wu, ffn_wd, ev_w_in, ev_w_out, lru_conv_w, lru_conv_b,
              lru_wa, lru_ba, lru_wx, lru_bx, lru_lambda, gla_w_gate, gla_b_gate, gla_norm,
              od_w_in, od_w_out, mlstm_conv_w, mlstm_conv_b, mlstm_wq, mlstm_wk, mlstm_wv,
              mlstm_w_if, mlstm_b_if, mlstm_skip, mlstm_norm, pool_w, pool_scale):
    for layer in range(DEPTH):
        g = norm_g[layer]
        j = layer // 2
        h = swiglu(rmsnorm(x, g[0]), ffn_wg[layer, 0], ffn_wu[layer, 0], ffn_wd[layer, 0])
        x = x + 0.5 * rmsnorm(h, g[1])
        u = rmsnorm(x, g[2])
        if layer % 2 == 0:
            m = even_mixer(u, ev_w_in[j], ev_w_out[j], lru_conv_w[j], lru_conv_b[j],
                           lru_wa[j], lru_ba[j], lru_wx[j], lru_bx[j], lru_lambda[j],
                           gla_w_gate[j], gla_b_gate[j], gla_norm[j])
        else:
            m = odd_mixer(u, od_w_in[j], od_w_out[j], mlstm_conv_w[j], mlstm_conv_b[j],
                          mlstm_wq[j], mlstm_wk[j], mlstm_wv[j], mlstm_w_if[j], mlstm_b_if[j],
                          mlstm_skip[j], mlstm_norm[j], pool_w[j], pool_scale[j])
        x = x + rmsnorm(m, g[3])
        h = swiglu(rmsnorm(x, g[4]), ffn_wg[layer, 1], ffn_wu[layer, 1], ffn_wd[layer, 1])
        x = x + 0.5 * rmsnorm(h, g[5])
    return x
```

```python
import functools

import jax
import jax.numpy as jnp
from jax import lax
from jax.experimental import pallas as pl
from jax.experimental.pallas import tpu as pltpu

F32 = jnp.float32
BF16 = jnp.bfloat16

EPS = 1e-6
LANES = 128
SUBLANES = 8
VMEM_LIMIT_BYTES = 56 * 1024 * 1024

CONV_W = 4
LRU_HEADS = 8
LRU_C = 8.0
GLA_HEADS = 4
GLA_TAU = 16.0
MLSTM_HEADS = 4
POOL_WINDOWS = (2, 4, 8, 16)
CHUNK = 64

FFN_ROW_TILE = 512
FFN_HID_TILE = 512
PROJ_ROW_TILE = 512
MIX_TIME_TILE = 256
CONV_HIST = SUBLANES
POOL_HIST = 16


def _rms(y, g):
    return y * lax.rsqrt(jnp.mean(y * y, axis=-1, keepdims=True) + EPS) * g


def _softplus(x):
    return jnp.maximum(x, 0.0) + jnp.log1p(jnp.exp(-jnp.abs(x)))


def _log_sigmoid(x):
    return -_softplus(-x)


def _dot(a, b):
    return jnp.dot(a, b, preferred_element_type=F32)


def _dot_nt(a, b):
    return lax.dot_general(a, b, (((1,), (1,)), ((), ())), preferred_element_type=F32)


def _dot_tn(a, b):
    return lax.dot_general(a, b, (((0,), (0,)), ((), ())), preferred_element_type=F32)


def _chunk_tril(n):
    row = lax.broadcasted_iota(jnp.int32, (n, n), 0)
    col = lax.broadcasted_iota(jnp.int32, (n, n), 1)
    same = (row // CHUNK) == (col // CHUNK)
    return jnp.where(same & (col <= row), 1.0, 0.0).astype(BF16)


def _chunk_cumsum(tril, x):
    x1 = x.astype(BF16)
    r1 = x - x1.astype(F32)
    x2 = r1.astype(BF16)
    x3 = (r1 - x2.astype(F32)).astype(BF16)
    return _dot(tril, x1) + _dot(tril, x2) + _dot(tril, x3)


def _ffn_body(x_ref, gpre_ref, wg_ref, wu_ref, wd_ref, gpost_ref, o_ref, xn_ref, acc_ref):
    j = pl.program_id(1)

    @pl.when(j == 0)
    def _():
        xn_ref[...] = _rms(x_ref[...], gpre_ref[...]).astype(BF16)
        acc_ref[...] = jnp.zeros_like(acc_ref)

    xn = xn_ref[...]
    a = _dot(xn, wg_ref[...])
    b = _dot(xn, wu_ref[...])
    h = (a * jax.nn.sigmoid(a) * b).astype(BF16)
    acc_ref[...] += _dot(h, wd_ref[...])

    @pl.when(j == pl.num_programs(1) - 1)
    def _():
        o_ref[...] = x_ref[...] + 0.5 * _rms(acc_ref[...], gpost_ref[...])


def _ffn(x, g_pre, wg, wu, wd, g_post):
    t, d = x.shape
    ff = wg.shape[1]
    tm, tf = min(FFN_ROW_TILE, t), FFN_HID_TILE
    assert t % tm == 0 and ff % tf == 0
    return pl.pallas_call(
        _ffn_body,
        out_shape=jax.ShapeDtypeStruct((t, d), F32),
        grid=(t // tm, ff // tf),
        in_specs=[
            pl.BlockSpec((tm, d), lambda i, j: (i, 0)),
            pl.BlockSpec((1, d), lambda i, j: (0, 0)),
            pl.BlockSpec((d, tf), lambda i, j: (0, j)),
            pl.BlockSpec((d, tf), lambda i, j: (0, j)),
            pl.BlockSpec((tf, d), lambda i, j: (j, 0)),
            pl.BlockSpec((1, d), lambda i, j: (0, 0)),
        ],
        out_specs=pl.BlockSpec((tm, d), lambda i, j: (i, 0)),
        scratch_shapes=[pltpu.VMEM((tm, d), BF16), pltpu.VMEM((tm, d), F32)],
        compiler_params=pltpu.CompilerParams(
            dimension_semantics=("parallel", "arbitrary"),
            vmem_limit_bytes=VMEM_LIMIT_BYTES),
        name="ffn",
    )(x, g_pre, wg, wu, wd, g_post)


def _inproj_body(x_ref, g_ref, w_ref, o_ref, xn_ref):
    @pl.when(pl.program_id(1) == 0)
    def _():
        xn_ref[...] = _rms(x_ref[...], g_ref[...]).astype(BF16)

    o_ref[...] = _dot(xn_ref[...], w_ref[...])


def _inproj(x, g, w, tn):
    t, d = x.shape
    n = w.shape[1]
    tm = min(PROJ_ROW_TILE, t)
    assert t % tm == 0 and n % tn == 0
    return pl.pallas_call(
        _inproj_body,
        out_shape=jax.ShapeDtypeStruct((t, n), F32),
        grid=(t // tm, n // tn),
        in_specs=[
            pl.BlockSpec((tm, d), lambda i, j: (i, 0)),
            pl.BlockSpec((1, d), lambda i, j: (0, 0)),
            pl.BlockSpec((d, tn), lambda i, j: (0, j)),
        ],
        out_specs=pl.BlockSpec((tm, tn), lambda i, j: (i, j)),
        scratch_shapes=[pltpu.VMEM((tm, d), BF16)],
        compiler_params=pltpu.CompilerParams(
            dimension_semantics=("parallel", "arbitrary"),
            vmem_limit_bytes=VMEM_LIMIT_BYTES),
        name="inproj",
    )(x, g, w)


def _outproj_body(m_ref, w_ref, g_ref, x_ref, o_ref):
    y = _dot(m_ref[...], w_ref[...])
    o_ref[...] = x_ref[...] + _rms(y, g_ref[...])


def _outproj(m, w, g, x):
    t, d = x.shape
    k = m.shape[1]
    tm = min(PROJ_ROW_TILE, t)
    assert t % tm == 0
    return pl.pallas_call(
        _outproj_body,
        out_shape=jax.ShapeDtypeStruct((t, d), F32),
        grid=(t // tm,),
        in_specs=[
            pl.BlockSpec((tm, k), lambda i: (i, 0)),
            pl.BlockSpec((k, d), lambda i: (0, 0)),
            pl.BlockSpec((1, d), lambda i: (0, 0)),
            pl.BlockSpec((tm, d), lambda i: (i, 0)),
        ],
        out_specs=pl.BlockSpec((tm, d), lambda i: (i, 0)),
        compiler_params=pltpu.CompilerParams(
            dimension_semantics=("parallel",),
            vmem_limit_bytes=VMEM_LIMIT_BYTES),
        name="outproj",
    )(m, w, g, x)


def _causal_conv(xs_ref, x, cw, cb, n):
    xs_ref[CONV_HIST:CONV_HIST + n, :] = x
    acc = cb
    for k in range(CONV_W):
        off = CONV_HIST - (CONV_W - 1) + k
        acc = acc + cw[k:k + 1, :] * xs_ref[off:off + n, :]
    xs_ref[0:CONV_HIST, :] = xs_ref[n:n + CONV_HIST, :]
    return acc


def _even_body(p_ref, cw_ref, cb_ref, wab_ref, ba_ref, bx_ref, lam_ref, gup_ref, gb_ref, gn_ref,
               o_ref, xs_ref, h_ref, st_ref, bc_ref, *, dh, dk, dv):
    n = o_ref.shape[0]
    d_half = LRU_HEADS * dh

    @pl.when(pl.program_id(1) == 0)
    def _():
        xs_ref[0:CONV_HIST, :] = jnp.zeros((CONV_HIST, d_half), F32)
        h_ref[...] = jnp.zeros_like(h_ref)
        st_ref[...] = jnp.zeros_like(st_ref)

    c_y = d_half
    c_q = 2 * d_half
    c_k = c_q + GLA_HEADS * dk
    c_v = c_k + GLA_HEADS * dk
    c_r = c_v + GLA_HEADS * dv
    c_g = c_r + GLA_HEADS * dv

    xc = _causal_conv(xs_ref, p_ref[:, 0:d_half], cw_ref[...], cb_ref[...], n)
    xcb = xc.astype(BF16)
    sp = _softplus(-lam_ref[...])
    row = lax.broadcasted_iota(jnp.int32, (n, dh), 0)
    for h in range(LRU_HEADS):
        sl = slice(h * dh, (h + 1) * dh)
        gates = _dot(xcb[:, sl], wab_ref[h])
        r = jax.nn.sigmoid(gates[:, 0:dh] + ba_ref[:, sl])
        i = jax.nn.sigmoid(gates[:, dh:2 * dh] + bx_ref[:, sl])
        log_a = -LRU_C * r * sp[:, sl]
        a = jnp.exp(log_a)
        b = jnp.sqrt(-jnp.tanh(log_a) * (a * a + 1.0)) * (i * xc[:, sl])
        d = 1
        while d < n:
            a_sh = jnp.where(row >= d, pltpu.roll(a, d, axis=0), 1.0)
            b_sh = jnp.where(row >= d, pltpu.roll(b, d, axis=0), 0.0)
            b = a * b_sh + b
            a = a * a_sh
            d *= 2
        hcol = b + a * h_ref[:, sl]
        h_ref[:, sl] = hcol[n - 1:n, :]
        y = p_ref[:, c_y + h * dh:c_y + (h + 1) * dh]
        o_ref[:, sl] = (hcol * jax.nn.gelu(y)).astype(o_ref.dtype)

    glr = p_ref[:, c_g:c_g + LANES].astype(BF16)
    gpre = _dot(glr, gup_ref[...]) + gb_ref[...]
    bc_ref[...] = _chunk_cumsum(_chunk_tril(n), _log_sigmoid(gpre) * (1.0 / GLA_TAU))
    crow = lax.broadcasted_iota(jnp.int32, (CHUNK, CHUNK), 0)
    ccol = lax.broadcasted_iota(jnp.int32, (CHUNK, CHUNK), 1)
    causal = ccol <= crow
    for c in range(n // CHUNK):
        rs = slice(c * CHUNK, (c + 1) * CHUNK)
        for h in range(GLA_HEADS):
            ks = slice(h * dk, (h + 1) * dk)
            bcum = bc_ref[rs, ks]
            q = p_ref[rs, c_q + h * dk:c_q + (h + 1) * dk]
            k = p_ref[rs, c_k + h * dk:c_k + (h + 1) * dk]
            v = p_ref[rs, c_v + h * dv:c_v + (h + 1) * dv].astype(BF16)
            rg = p_ref[rs, c_r + h * dv:c_r + (h + 1) * dv]
            qd = ((q * (dk ** -0.5)) * jnp.exp(bcum)).astype(BF16)
            ki = (k * jnp.exp(-bcum)).astype(BF16)
            s = jnp.where(causal, _dot_nt(qd, ki), 0.0).astype(BF16)
            st = st_ref[h]
            o = _dot(s, v) + _dot_nt(qd, st.astype(BF16))
            blast = bcum[CHUNK - 1:CHUNK, :]
            kd = (k * jnp.exp(blast - bcum)).astype(BF16)
            st_ref[h] = jnp.exp(blast) * st + _dot_tn(v, kd)
            on = _rms(o, gn_ref[:, h * dv:(h + 1) * dv]) * (rg * jax.nn.sigmoid(rg))
            o_ref[rs, d_half + h * dv:d_half + (h + 1) * dv] = on.astype(o_ref.dtype)


def _even_mixer(p, cw, cb, wab, ba, bx, lam, gup, gb, gn, *, d_half):
    bsz, s, ncols = p.shape
    n = min(MIX_TIME_TILE, s)
    assert s % n == 0 and n % CHUNK == 0
    dh = d_half // LRU_HEADS
    dk = gup.shape[1] // GLA_HEADS
    dv = d_half // GLA_HEADS
    rep2 = lambda b, t: (0, 0)
    rep3 = lambda b, t: (0, 0, 0)
    return pl.pallas_call(
        functools.partial(_even_body, dh=dh, dk=dk, dv=dv),
        out_shape=jax.ShapeDtypeStruct((bsz, s, 2 * d_half), BF16),
        grid=(bsz, s // n),
        in_specs=[
            pl.BlockSpec((None, n, ncols), lambda b, t: (b, t, 0)),
            pl.BlockSpec(cw.shape, rep2),
            pl.BlockSpec(cb.shape, rep2),
            pl.BlockSpec(wab.shape, rep3),
            pl.BlockSpec(ba.shape, rep2),
            pl.BlockSpec(bx.shape, rep2),
            pl.BlockSpec(lam.shape, rep2),
            pl.BlockSpec(gup.shape, rep2),
            pl.BlockSpec(gb.shape, rep2),
            pl.BlockSpec(gn.shape, rep2),
        ],
        out_specs=pl.BlockSpec((None, n, 2 * d_half), lambda b, t: (b, t, 0)),
        scratch_shapes=[
            pltpu.VMEM((n + CONV_HIST, d_half), F32),
            pltpu.VMEM((1, d_half), F32),
            pltpu.VMEM((GLA_HEADS, dv, dk), F32),
            pltpu.VMEM((n, GLA_HEADS * dk), F32),
        ],
        compiler_params=pltpu.CompilerParams(
            dimension_semantics=("parallel", "arbitrary"),
            vmem_limit_bytes=VMEM_LIMIT_BYTES),
        name="even_mixer",
    )(p, cw, cb, wab, ba, bx, lam, gup, gb, gn)


def _odd_body(p_ref, cw_ref, cb_ref, wq_ref, wk_ref, wv_ref, wif_ref, bif_ref, skip_ref, mn_ref,
              pw_ref, ps_ref, o_ref, xs_ref, us_ref, c_ref, n_ref, m_ref, xc_ref, q_ref, k_ref,
              v_ref, *, dh):
    n = o_ref.shape[0]
    d_half = MLSTM_HEADS * dh
    t = pl.program_id(1)

    @pl.when(t == 0)
    def _():
        xs_ref[0:CONV_HIST, :] = jnp.zeros((CONV_HIST, d_half), F32)
        us_ref[0:POOL_HIST, :] = jnp.zeros((POOL_HIST, d_half), F32)
        c_ref[...] = jnp.zeros_like(c_ref)
        n_ref[...] = jnp.zeros_like(n_ref)
        m_ref[...] = jnp.zeros_like(m_ref)

    xm = p_ref[:, 0:d_half]
    xc = _causal_conv(xs_ref, xm, cw_ref[...], cb_ref[...], n)
    xc = xc * jax.nn.sigmoid(xc)
    xc_ref[...] = xc
    xcb = xc.astype(BF16)
    xmb = xm.astype(BF16)
    gates = bif_ref[...]
    for h in range(MLSTM_HEADS):
        sl = slice(h * dh, (h + 1) * dh)
        q = _dot(xcb[:, sl], wq_ref[h])
        k = _dot(xcb[:, sl], wk_ref[h])
        v = _dot(xmb[:, sl], wv_ref[h])
        q_ref[:, sl] = q
        k_ref[:, sl] = k
        v_ref[:, sl] = v
        gates = (gates + _dot(q.astype(BF16), wif_ref[0, sl, :])
                 + _dot(k.astype(BF16), wif_ref[1, sl, :])
                 + _dot(v.astype(BF16), wif_ref[2, sl, :]))
    bcum_all = _chunk_cumsum(_chunk_tril(n), _log_sigmoid(gates))
    gates_t = gates.T
    bcum_t = bcum_all.T

    crow = lax.broadcasted_iota(jnp.int32, (CHUNK, CHUNK), 0)
    ccol = lax.broadcasted_iota(jnp.int32, (CHUNK, CHUNK), 1)
    causal = ccol <= crow
    for c in range(n // CHUNK):
        rs = slice(c * CHUNK, (c + 1) * CHUNK)
        for h in range(MLSTM_HEADS):
            sl = slice(h * dh, (h + 1) * dh)
            fcol = MLSTM_HEADS + h
            b_c = bcum_all[rs, fcol:fcol + 1]
            b_r = bcum_t[fcol:fcol + 1, rs]
            i_c = gates[rs, h:h + 1]
            i_r = gates_t[h:h + 1, rs]
            m_prev = m_ref[h]
            dmat = jnp.where(causal, b_c - b_r + i_r, -jnp.inf)
            inter = b_c + m_prev
            m_t = jnp.maximum(inter, jnp.max(dmat, axis=-1, keepdims=True))
            w = jnp.exp(dmat - m_t)
            inter_w = jnp.exp(inter - m_t)
            q = q_ref[rs, sl]
            qb = q.astype(BF16)
            k = k_ref[rs, sl] * (dh ** -0.5)
            kb = k.astype(BF16)
            v = v_ref[rs, sl]
            qk = _dot_nt(qb, kb) * w
            cst = c_ref[h]
            nst = n_ref[h]
            num = _dot(qk.astype(BF16), v.astype(BF16)) + inter_w * _dot_nt(qb, cst.astype(BF16))
            den = (jnp.sum(qk, axis=-1, keepdims=True)
                   + inter_w * jnp.sum(q * nst, axis=-1, keepdims=True))
            hid = num / jnp.maximum(jnp.abs(den), jnp.exp(-m_t))
            b_l = b_c[CHUNK - 1:CHUNK, :]
            g_c = b_l - b_c + i_c
            g_r = b_l - b_r + i_r
            m_new = jnp.maximum(b_l + m_prev, jnp.max(g_r, axis=-1, keepdims=True))
            decay = jnp.exp(b_l + m_prev - m_new)
            w_c = jnp.exp(g_c - m_new)
            c_ref[h] = decay * cst + _dot_tn((v * w_c).astype(BF16), kb)
            n_ref[h] = decay * nst + jnp.sum(k * w_c, axis=0, keepdims=True)
            m_ref[h] = m_new
            hn = _rms(hid, mn_ref[:, sl])
            z = p_ref[rs, d_half + h * dh:d_half + (h + 1) * dh]
            out = jax.nn.sigmoid(z) * (hn + skip_ref[:, sl] * xc_ref[rs, sl])
            o_ref[rs, sl] = out.astype(o_ref.dtype)

    u = p_ref[:, 2 * d_half:3 * d_half]
    us_ref[POOL_HIST:POOL_HIST + n, :] = u
    pos = t * n + lax.broadcasted_iota(jnp.int32, (n, 1), 0)
    gw = d_half // len(POOL_WINDOWS)
    for g, win in enumerate(POOL_WINDOWS):
        cs = slice(g * gw, (g + 1) * gw)
        s = us_ref[:, cs]
        d = 1
        while d < win:
            s = s + pltpu.roll(s, d, axis=0)
            d *= 2
        cnt = jnp.minimum(pos + 1, win).astype(F32)
        pooled = s[POOL_HIST:POOL_HIST + n, :] / cnt - u[:, cs]
        y = _dot(pooled.astype(BF16), pw_ref[g]) * ps_ref[:, cs]
        o_ref[:, d_half + g * gw:d_half + (g + 1) * gw] = y.astype(o_ref.dtype)
    us_ref[0:POOL_HIST, :] = us_ref[n:n + POOL_HIST, :]


def _odd_mixer(p, cw, cb, wq, wk, wv, wif, bif, skip, mn, pw, ps, *, d_half):
    bsz, s, ncols = p.shape
    n = min(MIX_TIME_TILE, s)
    assert s % n == 0 and n % CHUNK == 0
    dh = d_half // MLSTM_HEADS
    rep2 = lambda b, t: (0, 0)
    rep3 = lambda b, t: (0, 0, 0)
    return pl.pallas_call(
        functools.partial(_odd_body, dh=dh),
        out_shape=jax.ShapeDtypeStruct((bsz, s, 2 * d_half), BF16),
        grid=(bsz, s // n),
        in_specs=[
            pl.BlockSpec((None, n, ncols), lambda b, t: (b, t, 0)),
            pl.BlockSpec(cw.shape, rep2),
            pl.BlockSpec(cb.shape, rep2),
            pl.BlockSpec(wq.shape, rep3),
            pl.BlockSpec(wk.shape, rep3),
            pl.BlockSpec(wv.shape, rep3),
            pl.BlockSpec(wif.shape, rep3),
            pl.BlockSpec(bif.shape, rep2),
            pl.BlockSpec(skip.shape, rep2),
            pl.BlockSpec(mn.shape, rep2),
            pl.BlockSpec(pw.shape, rep3),
            pl.BlockSpec(ps.shape, rep2),
        ],
        out_specs=pl.BlockSpec((None, n, 2 * d_half), lambda b, t: (b, t, 0)),
        scratch_shapes=[
            pltpu.VMEM((n + CONV_HIST, d_half), F32),
            pltpu.VMEM((n + POOL_HIST, d_half), F32),
            pltpu.VMEM((MLSTM_HEADS, dh, dh), F32),
            pltpu.VMEM((MLSTM_HEADS, 1, dh), F32),
            pltpu.VMEM((MLSTM_HEADS, 1, 1), F32),
            pltpu.VMEM((n, d_half), F32),
            pltpu.VMEM((n, d_half), F32),
            pltpu.VMEM((n, d_half), F32),
            pltpu.VMEM((n, d_half), F32),
        ],
        compiler_params=pltpu.CompilerParams(
            dimension_semantics=("parallel", "arbitrary"),
            vmem_limit_bytes=VMEM_LIMIT_BYTES),
        name="odd_mixer",
    )(p, cw, cb, wq, wk, wv, wif, bif, skip, mn, pw, ps)


def _pad_cols(w, mult):
    pad = (-w.shape[-1]) % mult
    return jnp.pad(w, [(0, 0)] * (w.ndim - 1) + [(0, pad)])


def _blockdiag_dense(w, heads):
    nb, bs, _ = w.shape
    per = nb // heads
    eye = jnp.eye(per, dtype=w.dtype)
    dense = jnp.einsum('hgij,gk->hgikj', w.reshape(heads, per, bs, bs), eye)
    return dense.reshape(heads, per * bs, per * bs)


def kernel(x, norm_g, ffn_wg, ffn_wu, ffn_wd, ev_w_in, ev_w_out, lru_conv_w, lru_conv_b, lru_wa, lru_ba, lru_wx, lru_bx, lru_lambda, gla_w_gate, gla_b_gate, gla_norm, od_w_in, od_w_out, mlstm_conv_w, mlstm_conv_b, mlstm_wq, mlstm_wk, mlstm_wv, mlstm_w_if, mlstm_b_if, mlstm_skip, mlstm_norm, pool_w, pool_scale):
    bsz, seq, d = x.shape
    depth = norm_g.shape[0]
    d_half = lru_conv_w.shape[-1]
    t = bsz * seq
    row = lambda v: v.reshape(1, -1).astype(F32)

    xf = x.reshape(t, d)
    for layer in range(depth):
        g = norm_g[layer]
        j = layer // 2
        xf = _ffn(xf, row(g[0]), ffn_wg[layer, 0].astype(BF16), ffn_wu[layer, 0].astype(BF16),
                  ffn_wd[layer, 0].astype(BF16), row(g[1]))
        if layer % 2 == 0:
            w_in = _pad_cols(ev_w_in[j], 256).astype(BF16)
            p = _inproj(xf, row(g[2]), w_in, 768 if w_in.shape[1] % 768 == 0 else 256)
            wab = jnp.concatenate([lru_wa[j], lru_wx[j]], axis=-1).astype(BF16)
            rank = gla_w_gate.shape[1]
            gup = jnp.pad(gla_w_gate[j], ((0, LANES - rank), (0, 0))).astype(BF16)
            mixed = _even_mixer(
                p.reshape(bsz, seq, -1), lru_conv_w[j], row(lru_conv_b[j]), wab, row(lru_ba[j]),
                row(lru_bx[j]), row(lru_lambda[j]), gup, row(gla_b_gate[j]), row(gla_norm[j]),
                d_half=d_half)
            w_out = ev_w_out[j]
        else:
            w_in = od_w_in[j].astype(BF16)
            p = _inproj(xf, row(g[2]), w_in, 1024 if w_in.shape[1] % 1024 == 0 else 256)
            wq = _blockdiag_dense(mlstm_wq[j], MLSTM_HEADS).astype(BF16)
            wk = _blockdiag_dense(mlstm_wk[j], MLSTM_HEADS).astype(BF16)
            wv = _blockdiag_dense(mlstm_wv[j], MLSTM_HEADS).astype(BF16)
            wif = _pad_cols(mlstm_w_if[j], LANES).reshape(3, d_half, LANES).astype(BF16)
            bif = _pad_cols(mlstm_b_if[j].reshape(1, -1), LANES).astype(F32)
            mixed = _odd_mixer(
                p.reshape(bsz, seq, -1), mlstm_conv_w[j], row(mlstm_conv_b[j]), wq, wk, wv, wif, bif,
                row(mlstm_skip[j]), row(mlstm_norm[j]), pool_w[j].astype(BF16), row(pool_scale[j]),
                d_half=d_half)
            w_out = od_w_out[j]
        xf = _outproj(mixed.reshape(t, -1), w_out.astype(BF16), row(g[3]), xf)
        xf = _ffn(xf, row(g[4]), ffn_wg[layer, 1].astype(BF16), ffn_wu[layer, 1].astype(BF16),
                  ffn_wd[layer, 1].astype(BF16), row(g[5]))
    return xf.reshape(bsz, seq, d)
```

```python
import functools

import jax
import jax.numpy as jnp
from jax import lax
from jax.experimental import pallas as pl
from jax.experimental.pallas import tpu as pltpu

F32 = jnp.float32
BF16 = jnp.bfloat16

EPS = 1e-6
LANES = 128
SUBLANES = 8
MXU_DIM = 256
VMEM_LIMIT_BYTES = 56 * 1024 * 1024

CONV_W = 4
LRU_HEADS = 8
LRU_C = 8.0
GLA_HEADS = 4
GLA_TAU = 16.0
MLSTM_HEADS = 4
POOL_WINDOWS = (2, 4, 8, 16)
CHUNK = 64

FFN_ROW_TILE = 1024
FFN_HID_TILE = 512
FFN_EDGE_ROWS = 512
MIX_TIME_TILE = 256
CONV_HIST = SUBLANES
POOL_HIST = 16


def _rms(y, g):
    return y * lax.rsqrt(jnp.mean(y * y, axis=-1, keepdims=True) + EPS) * g


def _softplus(x):
    return jnp.maximum(x, 0.0) + jnp.log1p(jnp.exp(-jnp.abs(x)))


def _log_sigmoid(x):
    return -_softplus(-x)


def _dot(a, b):
    return jnp.dot(a, b, preferred_element_type=F32)


def _dot_nt(a, b):
    return lax.dot_general(a, b, (((1,), (1,)), ((), ())), preferred_element_type=F32)


def _dot_tn(a, b):
    return lax.dot_general(a, b, (((0,), (0,)), ((), ())), preferred_element_type=F32)


def _chunk_tril(n):
    row = lax.broadcasted_iota(jnp.int32, (n, n), 0)
    col = lax.broadcasted_iota(jnp.int32, (n, n), 1)
    same = (row // CHUNK) == (col // CHUNK)
    return jnp.where(same & (col <= row), 1.0, 0.0).astype(BF16)


def _chunk_cumsum(tril, x):
    x1 = x.astype(BF16)
    r1 = x - x1.astype(F32)
    x2 = r1.astype(BF16)
    x3 = (r1 - x2.astype(F32)).astype(BF16)
    return _dot(tril, x1) + _dot(tril, x2) + _dot(tril, x3)


def _proj_pieces(lhs, w_ref, p_refs, width):
    pieces = []
    c0 = 0
    for p_ref in p_refs:
        for l0 in range(0, p_ref.shape[1], width):
            def piece(p_ref=p_ref, l0=l0, c=c0 + l0):
                p_ref[:, l0:l0 + width] = _dot(lhs(), w_ref[:, c:c + width])
            pieces.append(piece)
        c0 += p_ref.shape[1]
    assert c0 == w_ref.shape[1]
    return pieces


def _order_after(dst_ref, value, zero_ref):
    rows = SUBLANES * (4 // dst_ref.dtype.itemsize)
    bits = pltpu.bitcast(dst_ref[0:rows, 0:LANES], jnp.int32)
    dep = pltpu.bitcast(value[0:SUBLANES, 0:LANES], jnp.int32) & zero_ref[...]
    dst_ref[0:rows, 0:LANES] = pltpu.bitcast(bits | dep, dst_ref.dtype)


def _run(pieces):
    for piece in pieces:
        piece()


def _causal_conv(xs_ref, x, cw, cb, n):
    xs_ref[CONV_HIST:CONV_HIST + n, :] = x
    acc = cb
    for k in range(CONV_W):
        off = CONV_HIST - (CONV_W - 1) + k
        acc = acc + cw[k:k + 1, :] * xs_ref[off:off + n, :]
    xs_ref[0:CONV_HIST, :] = xs_ref[n:n + CONV_HIST, :]
    return acc


def _linear_scan(a, b, carry):
    n, c = a.shape
    groups = n // SUBLANES
    a3 = a.reshape(groups, SUBLANES, c)
    b3 = b.reshape(groups, SUBLANES, c)
    sub = lax.broadcasted_iota(jnp.int32, (groups, SUBLANES, c), 1)
    d = 1
    while d < SUBLANES:
        a_sh = jnp.where(sub >= d, pltpu.roll(a3, d, axis=1), 1.0)
        b_sh = jnp.where(sub >= d, pltpu.roll(b3, d, axis=1), 0.0)
        b3 = a3 * b_sh + b3
        a3 = a3 * a_sh
        d *= 2
    outs = []
    for v in range(groups):
        hv = b3[v] + a3[v] * carry
        outs.append(hv)
        carry = hv[SUBLANES - 1:SUBLANES, :]
    return jnp.concatenate(outs, axis=0), carry


def _ffn_step(x_ref, gpre_ref, wg_ref, wu_ref, wd_ref, gpost_ref, o_ref, xn_ref, *,
              first, final, bounds):
    acc_ref = o_ref
    for r0, r1 in zip(bounds[:-1], bounds[1:]):
        rs = slice(r0, r1)
        if first:
            xn_ref[rs, :] = _rms(x_ref[rs, :], gpre_ref[...]).astype(BF16)
        xn = xn_ref[rs, :]
        a = _dot(xn, wg_ref[...])
        b = _dot(xn, wu_ref[...])
        h = (a * jax.nn.sigmoid(a) * b).astype(BF16)
        upd = _dot(h, wd_ref[...])
        acc = upd if first else acc_ref[rs, :] + upd
        if final:
            o_ref[rs, :] = x_ref[rs, :] + _rms(acc, gpost_ref[...])
        else:
            acc_ref[rs, :] = acc


def _ffn_body(*refs):
    j = pl.program_id(1)
    last = pl.num_programs(1) - 1
    tm = refs[0].shape[0]
    pl.when(j == 0)(functools.partial(_ffn_step, *refs, first=True, final=False,
                                      bounds=(0, FFN_EDGE_ROWS, tm)))
    pl.when(jnp.logical_and(j > 0, j < last))(
        functools.partial(_ffn_step, *refs, first=False, final=False, bounds=(0, tm)))
    pl.when(j == last)(functools.partial(_ffn_step, *refs, first=False, final=True,
                                         bounds=(0, tm - FFN_EDGE_ROWS, tm)))


def _ffn(x, g_pre, wg, wu, wd, g_post_half):
    t, d = x.shape
    ff = wg.shape[1]
    tm, tf = min(FFN_ROW_TILE, t), FFN_HID_TILE
    assert t % tm == 0 and ff % tf == 0 and ff // tf >= 2 and tm > FFN_EDGE_ROWS
    return pl.pallas_call(
        _ffn_body,
        out_shape=jax.ShapeDtypeStruct((t, d), F32),
        grid=(t // tm, ff // tf),
        in_specs=[
            pl.BlockSpec((tm, d), lambda i, j: (i, 0)),
            pl.BlockSpec((1, d), lambda i, j: (0, 0)),
            pl.BlockSpec((d, tf), lambda i, j: (0, j)),
            pl.BlockSpec((d, tf), lambda i, j: (0, j)),
            pl.BlockSpec((tf, d), lambda i, j: (j, 0)),
            pl.BlockSpec((1, d), lambda i, j: (0, 0)),
        ],
        out_specs=pl.BlockSpec((tm, d), lambda i, j: (i, 0)),
        scratch_shapes=[pltpu.VMEM((tm, d), BF16)],
        compiler_params=pltpu.CompilerParams(
            dimension_semantics=("parallel", "arbitrary"),
            vmem_limit_bytes=VMEM_LIMIT_BYTES),
        name="ffn",
    )(x, g_pre, wg, wu, wd, g_post_half)


def _even_body(x_ref, g2_ref, win_ref, cw_ref, cb_ref, wab_ref, ba_ref, bx_ref, lam_ref, gup_ref,
               gb_ref, gn_ref, wout_ref, g3_ref, zero_ref, o_ref, px_ref, py_ref, pqk_ref, pv_ref,
               pr_ref, pg_ref, mixa_ref, mixb_ref, xs_ref, h_ref, st_ref, bc_ref, ma_ref, xn_ref,
               gates_ref, *, dh, dk, dv):
    n = o_ref.shape[0]
    d_half = LRU_HEADS * dh

    @pl.when(pl.program_id(1) == 0)
    def _():
        xs_ref[0:CONV_HIST, :] = jnp.zeros((CONV_HIST, d_half), F32)
        h_ref[...] = jnp.zeros_like(h_ref)
        st_ref[...] = jnp.zeros_like(st_ref)

    x = x_ref[...]
    xn_ref[...] = _rms(x, g2_ref[...]).astype(BF16)
    proj = _proj_pieces(lambda: xn_ref[...], win_ref,
                        (px_ref, py_ref, pqk_ref, pv_ref, pr_ref, pg_ref), MXU_DIM)
    per_group = d_half // MXU_DIM
    _run(proj[0:per_group])
    c_k = GLA_HEADS * dk

    xc = _causal_conv(xs_ref, px_ref[...], cw_ref[...], cb_ref[...], n)
    _run(proj[per_group:2 * per_group])
    xcb = xc.astype(BF16)
    sp = _softplus(-lam_ref[...])
    for h in range(LRU_HEADS):
        gates_ref[:, 2 * h * dh:2 * (h + 1) * dh] = _dot(xcb[:, h * dh:(h + 1) * dh], wab_ref[h])
    _order_after(xn_ref, gates_ref[:, 2 * (LRU_HEADS - 1) * dh:2 * LRU_HEADS * dh], zero_ref)
    rest = proj[-1:] + proj[2 * per_group:-1]
    share = -(-len(rest) // LRU_HEADS)
    for h in range(LRU_HEADS):
        _run(rest[h * share:(h + 1) * share])
        sl = slice(h * dh, (h + 1) * dh)
        gates = gates_ref[:, 2 * h * dh:2 * (h + 1) * dh]
        r = jax.nn.sigmoid(gates[:, 0:dh] + ba_ref[:, sl])
        i = jax.nn.sigmoid(gates[:, dh:2 * dh] + bx_ref[:, sl])
        log_a = -LRU_C * r * sp[:, sl]
        a = jnp.exp(log_a)
        b = jnp.sqrt(-jnp.tanh(log_a) * (a * a + 1.0)) * (i * xc[:, sl])
        hcol, h_ref[:, sl] = _linear_scan(a, b, h_ref[:, sl])
        mixa_ref[:, sl] = (hcol * jax.nn.gelu(py_ref[:, sl])).astype(mixa_ref.dtype)

    glr = pg_ref[:, 0:LANES].astype(BF16)
    gpre = _dot(glr, gup_ref[...]) + gb_ref[...]
    bc_ref[...] = _chunk_cumsum(_chunk_tril(n), _log_sigmoid(gpre) * (1.0 / GLA_TAU))
    crow = lax.broadcasted_iota(jnp.int32, (CHUNK, CHUNK), 0)
    ccol = lax.broadcasted_iota(jnp.int32, (CHUNK, CHUNK), 1)
    causal = ccol <= crow
    n_chunks = n // CHUNK
    ow = wout_ref.shape[1] // n_chunks
    for c in range(n_chunks):
        ma_ref[:, c * ow:(c + 1) * ow] = _dot(mixa_ref[...], wout_ref[0:d_half, c * ow:(c + 1) * ow])
        rs = slice(c * CHUNK, (c + 1) * CHUNK)
        for h in range(GLA_HEADS):
            ks = slice(h * dk, (h + 1) * dk)
            bcum = bc_ref[rs, ks]
            q = pqk_ref[rs, ks]
            k = pqk_ref[rs, c_k + h * dk:c_k + (h + 1) * dk]
            v = pv_ref[rs, h * dv:(h + 1) * dv].astype(BF16)
            rg = pr_ref[rs, h * dv:(h + 1) * dv]
            qd = ((q * (dk ** -0.5)) * jnp.exp(bcum)).astype(BF16)
            ki = (k * jnp.exp(-bcum)).astype(BF16)
            s = jnp.where(causal, _dot_nt(qd, ki), 0.0).astype(BF16)
            st = st_ref[h]
            o = _dot(s, v) + _dot_nt(qd, st.astype(BF16))
            blast = bcum[CHUNK - 1:CHUNK, :]
            kd = (k * jnp.exp(blast - bcum)).astype(BF16)
            st_ref[h] = jnp.exp(blast) * st + _dot_tn(v, kd)
            on = _rms(o, gn_ref[:, h * dv:(h + 1) * dv]) * (rg * jax.nn.sigmoid(rg))
            mixb_ref[rs, h * dv:(h + 1) * dv] = on.astype(mixb_ref.dtype)

    m = ma_ref[...] + _dot(mixb_ref[...], wout_ref[d_half:, :])
    o_ref[...] = x + _rms(m, g3_ref[...])


def _full(a):
    return pl.BlockSpec(a.shape, lambda b, t: (0,) * a.ndim)


def _even_layer(x, g2, w_in, cw, cb, wab, ba, bx, lam, gup, gb, gn, w_out, g3):
    bsz, s, d = x.shape
    d_half = cw.shape[1]
    n = min(MIX_TIME_TILE, s)
    assert s % n == 0 and n % CHUNK == 0 and w_in.shape[1] % MXU_DIM == 0
    dh = d_half // LRU_HEADS
    dk = gup.shape[1] // GLA_HEADS
    dv = d_half // GLA_HEADS
    zero = jnp.zeros((SUBLANES, LANES), jnp.int32)
    params = (g2, w_in, cw, cb, wab, ba, bx, lam, gup, gb, gn, w_out, g3, zero)
    tile = pl.BlockSpec((None, n, d), lambda b, t: (b, t, 0))
    return pl.pallas_call(
        functools.partial(_even_body, dh=dh, dk=dk, dv=dv),
        out_shape=jax.ShapeDtypeStruct((bsz, s, d), F32),
        grid=(bsz, s // n),
        in_specs=[tile] + [_full(a) for a in params],
        out_specs=tile,
        scratch_shapes=[
            pltpu.VMEM((n, d_half), F32),
            pltpu.VMEM((n, d_half), F32),
            pltpu.VMEM((n, 2 * GLA_HEADS * dk), F32),
            pltpu.VMEM((n, d_half), F32),
            pltpu.VMEM((n, d_half), F32),
            pltpu.VMEM((n, w_in.shape[1] - 4 * d_half - 2 * GLA_HEADS * dk), F32),
            pltpu.VMEM((n, d_half), BF16),
            pltpu.VMEM((n, d_half), BF16),
            pltpu.VMEM((n + CONV_HIST, d_half), F32),
            pltpu.VMEM((1, d_half), F32),
            pltpu.VMEM((GLA_HEADS, dv, dk), F32),
            pltpu.VMEM((n, GLA_HEADS * dk), F32),
            pltpu.VMEM((n, d), F32),
            pltpu.VMEM((n, d), BF16),
            pltpu.VMEM((n, 2 * d_half), F32),
        ],
        compiler_params=pltpu.CompilerParams(
            dimension_semantics=("parallel", "arbitrary"),
            vmem_limit_bytes=VMEM_LIMIT_BYTES),
        name="even_layer",
    )(x, *params)


def _odd_body(x_ref, g2_ref, win_ref, cw_ref, cb_ref, wq_ref, wk_ref, wv_ref, wif_ref, bif_ref,
              skip_ref, mn_ref, pw_ref, ps_ref, wout_ref, g3_ref, o_ref, pxm_ref, pz_ref, pu_ref,
              hn_ref, mixb_ref, mb_ref, xs_ref, us_ref, c_ref, n_ref, m_ref, xc_ref, q_ref, k_ref,
              v_ref, *, dh):
    n = o_ref.shape[0]
    d_half = MLSTM_HEADS * dh
    t = pl.program_id(1)

    @pl.when(t == 0)
    def _():
        xs_ref[0:CONV_HIST, :] = jnp.zeros((CONV_HIST, d_half), F32)
        us_ref[0:POOL_HIST, :] = jnp.zeros((POOL_HIST, d_half), F32)
        c_ref[...] = jnp.zeros_like(c_ref)
        n_ref[...] = jnp.zeros_like(n_ref)
        m_ref[...] = jnp.zeros_like(m_ref)

    x = x_ref[...]
    xn = _rms(x, g2_ref[...]).astype(BF16)
    proj = _proj_pieces(lambda: xn, win_ref, (pxm_ref, pz_ref, pu_ref), MXU_DIM)
    per_group = d_half // MXU_DIM
    _run(proj[0:per_group])

    xm = pxm_ref[...]
    xc = _causal_conv(xs_ref, xm, cw_ref[...], cb_ref[...], n)
    _run(proj[2 * per_group:3 * per_group])
    xc = xc * jax.nn.sigmoid(xc)
    xc_ref[...] = xc
    xcb = xc.astype(BF16)
    xmb = xm.astype(BF16)
    u = pu_ref[...]
    us_ref[POOL_HIST:POOL_HIST + n, :] = u
    pos = t * n + lax.broadcasted_iota(jnp.int32, (n, 1), 0)
    gw = d_half // len(POOL_WINDOWS)
    assert len(POOL_WINDOWS) == MLSTM_HEADS
    gates = bif_ref[...]
    for h in range(MLSTM_HEADS):
        cs = slice(h * gw, (h + 1) * gw)
        s = us_ref[:, cs]
        d = 1
        while d < POOL_WINDOWS[h]:
            s = s + pltpu.roll(s, d, axis=0)
            d *= 2
        cnt = jnp.minimum(pos + 1, POOL_WINDOWS[h]).astype(F32)
        pooled = s[POOL_HIST:POOL_HIST + n, :] / cnt - u[:, cs]
        y = _dot(pooled.astype(BF16), pw_ref[h]) * ps_ref[:, cs]
        mixb_ref[:, cs] = y.astype(mixb_ref.dtype)

        sl = slice(h * dh, (h + 1) * dh)
        q = _dot(xcb[:, sl], wq_ref[h])
        k = _dot(xcb[:, sl], wk_ref[h])
        v = _dot(xmb[:, sl], wv_ref[h])
        q_ref[:, sl] = q
        k_ref[:, sl] = k
        v_ref[:, sl] = v
        gates = (gates + _dot(q.astype(BF16), wif_ref[0, sl, :])
                 + _dot(k.astype(BF16), wif_ref[1, sl, :])
                 + _dot(v.astype(BF16), wif_ref[2, sl, :]))
    bcum_all = _chunk_cumsum(_chunk_tril(n), _log_sigmoid(gates))
    gates_t = gates.T
    bcum_t = bcum_all.T
    us_ref[0:POOL_HIST, :] = us_ref[n:n + POOL_HIST, :]

    crow = lax.broadcasted_iota(jnp.int32, (CHUNK, CHUNK), 0)
    ccol = lax.broadcasted_iota(jnp.int32, (CHUNK, CHUNK), 1)
    causal = ccol <= crow
    n_chunks = n // CHUNK
    zpieces = proj[per_group:2 * per_group]
    zshare = -(-len(zpieces) // n_chunks)
    ow = wout_ref.shape[1] // n_chunks
    for c in range(n_chunks):
        _run(zpieces[c * zshare:(c + 1) * zshare])
        mb_ref[:, c * ow:(c + 1) * ow] = _dot(mixb_ref[...], wout_ref[d_half:, c * ow:(c + 1) * ow])
        rs = slice(c * CHUNK, (c + 1) * CHUNK)
        for h in range(MLSTM_HEADS):
            sl = slice(h * dh, (h + 1) * dh)
            fcol = MLSTM_HEADS + h
            b_c = bcum_all[rs, fcol:fcol + 1]
            b_r = bcum_t[fcol:fcol + 1, rs]
            i_c = gates[rs, h:h + 1]
            i_r = gates_t[h:h + 1, rs]
            m_prev = m_ref[h]
            dmat = jnp.where(causal, b_c - b_r + i_r, -jnp.inf)
            inter = b_c + m_prev
            m_t = jnp.maximum(inter, jnp.max(dmat, axis=-1, keepdims=True))
            w = jnp.exp(dmat - m_t)
            inter_w = jnp.exp(inter - m_t)
            q = q_ref[rs, sl]
            qb = q.astype(BF16)
            k = k_ref[rs, sl] * (dh ** -0.5)
            kb = k.astype(BF16)
            v = v_ref[rs, sl]
            qk = _dot_nt(qb, kb) * w
            cst = c_ref[h]
            nst = n_ref[h]
            num = _dot(qk.astype(BF16), v.astype(BF16)) + inter_w * _dot_nt(qb, cst.astype(BF16))
            den = (jnp.sum(qk, axis=-1, keepdims=True)
                   + inter_w * jnp.sum(q * nst, axis=-1, keepdims=True))
            hid = num / jnp.maximum(jnp.abs(den), jnp.exp(-m_t))
            b_l = b_c[CHUNK - 1:CHUNK, :]
            g_c = b_l - b_c + i_c
            g_r = b_l - b_r + i_r
            m_new = jnp.maximum(b_l + m_prev, jnp.max(g_r, axis=-1, keepdims=True))
            decay = jnp.exp(b_l + m_prev - m_new)
            w_c = jnp.exp(g_c - m_new)
            c_ref[h] = decay * cst + _dot_tn((v * w_c).astype(BF16), kb)
            n_ref[h] = decay * nst + jnp.sum(k * w_c, axis=0, keepdims=True)
            m_ref[h] = m_new
            hn_ref[rs, sl] = _rms(hid, mn_ref[:, sl])

    m = mb_ref[...]
    for h in range(MLSTM_HEADS):
        sl = slice(h * dh, (h + 1) * dh)
        out = jax.nn.sigmoid(pz_ref[:, sl]) * (hn_ref[:, sl] + skip_ref[:, sl] * xc_ref[:, sl])
        m = m + _dot(out.astype(BF16), wout_ref[sl, :])
    o_ref[...] = x + _rms(m, g3_ref[...])


def _odd_layer(x, g2, w_in, cw, cb, wq, wk, wv, wif, bif, skip, mn, pw, ps, w_out, g3):
    bsz, s, d = x.shape
    d_half = cw.shape[1]
    n = min(MIX_TIME_TILE, s)
    assert s % n == 0 and n % CHUNK == 0 and w_in.shape[1] == 3 * d_half
    dh = d_half // MLSTM_HEADS
    params = (g2, w_in, cw, cb, wq, wk, wv, wif, bif, skip, mn, pw, ps, w_out, g3)
    tile = pl.BlockSpec((None, n, d), lambda b, t: (b, t, 0))
    return pl.pallas_call(
        functools.partial(_odd_body, dh=dh),
        out_shape=jax.ShapeDtypeStruct((bsz, s, d), F32),
        grid=(bsz, s // n),
        in_specs=[tile] + [_full(a) for a in params],
        out_specs=tile,
        scratch_shapes=[
            pltpu.VMEM((n, d_half), F32),
            pltpu.VMEM((n, d_half), F32),
            pltpu.VMEM((n, d_half), F32),
            pltpu.VMEM((n, d_half), F32),
            pltpu.VMEM((n, d_half), BF16),
            pltpu.VMEM((n, d), F32),
            pltpu.VMEM((n + CONV_HIST, d_half), F32),
            pltpu.VMEM((n + POOL_HIST, d_half), F32),
            pltpu.VMEM((MLSTM_HEADS, dh, dh), F32),
            pltpu.VMEM((MLSTM_HEADS, 1, dh), F32),
            pltpu.VMEM((MLSTM_HEADS, 1, 1), F32),
            pltpu.VMEM((n, d_half), F32),
            pltpu.VMEM((n, d_half), F32),
            pltpu.VMEM((n, d_half), F32),
            pltpu.VMEM((n, d_half), F32),
        ],
        compiler_params=pltpu.CompilerParams(
            dimension_semantics=("parallel", "arbitrary"),
            vmem_limit_bytes=VMEM_LIMIT_BYTES),
        name="odd_layer",
    )(x, *params)


def _pad_cols(w, mult):
    pad = (-w.shape[-1]) % mult
    return jnp.pad(w, [(0, 0)] * (w.ndim - 1) + [(0, pad)])


def _blockdiag_dense(w, heads):
    nb, bs, _ = w.shape
    per = nb // heads
    eye = jnp.eye(per, dtype=w.dtype)
    dense = jnp.einsum('hgij,gk->hgikj', w.reshape(heads, per, bs, bs), eye)
    return dense.reshape(heads, per * bs, per * bs)


def kernel(x, norm_g, ffn_wg, ffn_wu, ffn_wd, ev_w_in, ev_w_out, lru_conv_w, lru_conv_b, lru_wa, lru_ba, lru_wx, lru_bx, lru_lambda, gla_w_gate, gla_b_gate, gla_norm, od_w_in, od_w_out, mlstm_conv_w, mlstm_conv_b, mlstm_wq, mlstm_wk, mlstm_wv, mlstm_w_if, mlstm_b_if, mlstm_skip, mlstm_norm, pool_w, pool_scale):
    bsz, seq, d = x.shape
    depth = norm_g.shape[0]
    d_half = lru_conv_w.shape[-1]
    t = bsz * seq
    row = lambda v: v.reshape(1, -1).astype(F32)

    def ffn(xs, layer, which, g_pre, g_post):
        y = _ffn(xs.reshape(t, d), row(g_pre), ffn_wg[layer, which].astype(BF16),
                 ffn_wu[layer, which].astype(BF16), ffn_wd[layer, which].astype(BF16),
                 row(0.5 * g_post))
        return y.reshape(bsz, seq, d)

    for layer in range(depth):
        g = norm_g[layer]
        j = layer // 2
        x = ffn(x, layer, 0, g[0], g[1])
        if layer % 2 == 0:
            w_in = _pad_cols(ev_w_in[j], MXU_DIM).astype(BF16)
            wab = jnp.concatenate([lru_wa[j], lru_wx[j]], axis=-1).astype(BF16)
            rank = gla_w_gate.shape[1]
            gup = jnp.pad(gla_w_gate[j], ((0, LANES - rank), (0, 0))).astype(BF16)
            x = _even_layer(
                x, row(g[2]), w_in, lru_conv_w[j], row(lru_conv_b[j]), wab, row(lru_ba[j]),
                row(lru_bx[j]), row(lru_lambda[j]), gup, row(gla_b_gate[j]), row(gla_norm[j]),
                ev_w_out[j].astype(BF16), row(g[3]))
        else:
            wq = _blockdiag_dense(mlstm_wq[j], MLSTM_HEADS).astype(BF16)
            wk = _blockdiag_dense(mlstm_wk[j], MLSTM_HEADS).astype(BF16)
            wv = _blockdiag_dense(mlstm_wv[j], MLSTM_HEADS).astype(BF16)
            wif = _pad_cols(mlstm_w_if[j], LANES).reshape(3, d_half, LANES).astype(BF16)
            bif = _pad_cols(mlstm_b_if[j].reshape(1, -1), LANES).astype(F32)
            x = _odd_layer(
                x, row(g[2]), od_w_in[j].astype(BF16), mlstm_conv_w[j], row(mlstm_conv_b[j]),
                wq, wk, wv, wif, bif, row(mlstm_skip[j]), row(mlstm_norm[j]),
                pool_w[j].astype(BF16), row(pool_scale[j]), od_w_out[j].astype(BF16), row(g[3]))
        x = ffn(x, layer, 1, g[4], g[5])
    return x
```

```python
import functools

import jax
import jax.numpy as jnp
from jax import lax
from jax.experimental import pallas as pl
from jax.experimental.pallas import tpu as pltpu

F32 = jnp.float32
BF16 = jnp.bfloat16

EPS = 1e-6
LANES = 128
SUBLANES = 8
MXU_DIM = 256
VMEM_LIMIT_BYTES = 56 * 1024 * 1024

CONV_W = 4
LRU_HEADS = 8
LRU_C = 8.0
GLA_HEADS = 4
GLA_TAU = 16.0
MLSTM_HEADS = 4
POOL_WINDOWS = (2, 4, 8, 16)
CHUNK = 64

FFN_ROW_TILE = 1024
FFN_HID_TILE = 512
FFN_EDGE_ROWS = 512
MIX_TIME_TILE = 256
PROJ_PIECE = MXU_DIM
CONV_HIST = SUBLANES
POOL_HIST = 16


def _rms(y, g):
    return y * lax.rsqrt(jnp.mean(y * y, axis=-1, keepdims=True) + EPS) * g


def _softplus(x):
    return jnp.maximum(x, 0.0) + jnp.log1p(jnp.exp(-jnp.abs(x)))


def _log_sigmoid(x):
    return -_softplus(-x)


def _dot(a, b):
    return jnp.dot(a, b, preferred_element_type=F32)


def _dot_nt(a, b):
    return lax.dot_general(a, b, (((1,), (1,)), ((), ())), preferred_element_type=F32)


def _dot_tn(a, b):
    return lax.dot_general(a, b, (((0,), (0,)), ((), ())), preferred_element_type=F32)


def _chunk_tril(n):
    row = lax.broadcasted_iota(jnp.int32, (n, n), 0)
    col = lax.broadcasted_iota(jnp.int32, (n, n), 1)
    same = (row // CHUNK) == (col // CHUNK)
    return jnp.where(same & (col <= row), 1.0, 0.0).astype(BF16)


def _chunk_cumsum(tril, x):
    x1 = x.astype(BF16)
    r1 = x - x1.astype(F32)
    x2 = r1.astype(BF16)
    x3 = (r1 - x2.astype(F32)).astype(BF16)
    return _dot(tril, x1) + _dot(tril, x2) + _dot(tril, x3)


def _proj_pieces(lhs, w_ref, p_refs, width):
    pieces = []
    c0 = 0
    for p_ref in p_refs:
        for l0 in range(0, p_ref.shape[1], width):
            def piece(p_ref=p_ref, l0=l0, c=c0 + l0, w=min(width, p_ref.shape[1] - l0)):
                p_ref[:, l0:l0 + w] = _dot(lhs(), w_ref[:, c:c + w])
            pieces.append(piece)
        c0 += p_ref.shape[1]
    assert c0 == w_ref.shape[1]
    return pieces


def _run(pieces):
    for piece in pieces:
        piece()


def _order_after(dst_ref, value, zero_ref):
    rows = SUBLANES * (4 // dst_ref.dtype.itemsize)
    bits = pltpu.bitcast(dst_ref[0:rows, 0:LANES], jnp.int32)
    dep = pltpu.bitcast(value[0:SUBLANES, 0:LANES], jnp.int32) & zero_ref[...]
    dst_ref[0:rows, 0:LANES] = pltpu.bitcast(bits | dep, dst_ref.dtype)


def _causal_conv(xs_ref, x, cw, cb, n):
    xs_ref[CONV_HIST:CONV_HIST + n, :] = x
    acc = cb
    for k in range(CONV_W):
        off = CONV_HIST - (CONV_W - 1) + k
        acc = acc + cw[k:k + 1, :] * xs_ref[off:off + n, :]
    xs_ref[0:CONV_HIST, :] = xs_ref[n:n + CONV_HIST, :]
    return acc


def _linear_scan(a, b, carry):
    n, c = a.shape
    groups = n // SUBLANES
    a3 = a.reshape(groups, SUBLANES, c)
    b3 = b.reshape(groups, SUBLANES, c)
    sub = lax.broadcasted_iota(jnp.int32, (groups, SUBLANES, c), 1)
    d = 1
    while d < SUBLANES:
        a_sh = jnp.where(sub >= d, pltpu.roll(a3, d, axis=1), 1.0)
        b_sh = jnp.where(sub >= d, pltpu.roll(b3, d, axis=1), 0.0)
        b3 = a3 * b_sh + b3
        a3 = a3 * a_sh
        d *= 2
    outs = []
    for v in range(groups):
        hv = b3[v] + a3[v] * carry
        outs.append(hv)
        carry = hv[SUBLANES - 1:SUBLANES, :]
    return jnp.concatenate(outs, axis=0), carry


def _ffn_step(x_ref, gpre_ref, wg_ref, wu_ref, wd_ref, gpost_ref, o_ref, xn_ref, *,
              first, final, bounds):
    acc_ref = o_ref
    for r0, r1 in zip(bounds[:-1], bounds[1:]):
        rs = slice(r0, r1)
        if first:
            xn_ref[rs, :] = _rms(x_ref[rs, :], gpre_ref[...]).astype(BF16)
        xn = xn_ref[rs, :]
        a = _dot(xn, wg_ref[...])
        b = _dot(xn, wu_ref[...])
        h = (a * jax.nn.sigmoid(a) * b).astype(BF16)
        upd = _dot(h, wd_ref[...])
        acc = upd if first else acc_ref[rs, :] + upd
        if final:
            o_ref[rs, :] = x_ref[rs, :] + _rms(acc, gpost_ref[...])
        else:
            acc_ref[rs, :] = acc


def _ffn_body(*refs):
    j = pl.program_id(1)
    last = pl.num_programs(1) - 1
    tm = refs[0].shape[0]
    pl.when(j == 0)(functools.partial(_ffn_step, *refs, first=True, final=False,
                                      bounds=(0, FFN_EDGE_ROWS, tm)))
    pl.when(jnp.logical_and(j > 0, j < last))(
        functools.partial(_ffn_step, *refs, first=False, final=False, bounds=(0, tm)))
    pl.when(j == last)(functools.partial(_ffn_step, *refs, first=False, final=True,
                                         bounds=(0, tm - FFN_EDGE_ROWS, tm)))


def _ffn(x, g_pre, wg, wu, wd, g_post_half, layer, which):
    t, d = x.shape
    ff = wg.shape[-1]
    tm, tf = min(FFN_ROW_TILE, t), FFN_HID_TILE
    assert t % tm == 0 and ff % tf == 0 and ff // tf >= 2 and tm > FFN_EDGE_ROWS
    return pl.pallas_call(
        _ffn_body,
        out_shape=jax.ShapeDtypeStruct((t, d), F32),
        grid=(t // tm, ff // tf),
        in_specs=[
            pl.BlockSpec((tm, d), lambda i, j: (i, 0)),
            pl.BlockSpec((1, d), lambda i, j: (0, 0)),
            pl.BlockSpec((None, None, d, tf), lambda i, j: (layer, which, 0, j)),
            pl.BlockSpec((None, None, d, tf), lambda i, j: (layer, which, 0, j)),
            pl.BlockSpec((None, None, tf, d), lambda i, j: (layer, which, j, 0)),
            pl.BlockSpec((1, d), lambda i, j: (0, 0)),
        ],
        out_specs=pl.BlockSpec((tm, d), lambda i, j: (i, 0)),
        scratch_shapes=[pltpu.VMEM((tm, d), BF16)],
        compiler_params=pltpu.CompilerParams(
            dimension_semantics=("parallel", "arbitrary"),
            vmem_limit_bytes=VMEM_LIMIT_BYTES),
        name="ffn",
    )(x, g_pre, wg, wu, wd, g_post_half)


def _even_body(x_ref, g2_ref, win_ref, cw_ref, cb_ref, wab_ref, ba_ref, bx_ref, lam_ref, gup_ref,
               gb_ref, gn_ref, wout_ref, g3_ref, zero_ref, o_ref, px_ref, py_ref, pqk_ref, pv_ref,
               pr_ref, pg_ref, mixa_ref, mixb_ref, xs_ref, h_ref, st_ref, bc_ref, ma_ref, xn_ref,
               gates_ref, *, dh, dk, dv):
    n = o_ref.shape[0]
    d_half = LRU_HEADS * dh

    @pl.when(pl.program_id(1) == 0)
    def _():
        xs_ref[0:CONV_HIST, :] = jnp.zeros((CONV_HIST, d_half), F32)
        h_ref[...] = jnp.zeros_like(h_ref)
        st_ref[...] = jnp.zeros_like(st_ref)

    x = x_ref[...]
    xn_ref[...] = _rms(x, g2_ref[...]).astype(BF16)
    proj = _proj_pieces(lambda: xn_ref[...], win_ref,
                        (px_ref, py_ref, pqk_ref, pv_ref, pr_ref, pg_ref), PROJ_PIECE)
    per_group = d_half // PROJ_PIECE
    _run(proj[0:per_group])
    c_k = GLA_HEADS * dk

    xc = _causal_conv(xs_ref, px_ref[...], cw_ref[...], cb_ref[...], n)
    _run(proj[per_group:2 * per_group])
    xcb = xc.astype(BF16)
    sp = _softplus(-lam_ref[...])
    for h in range(LRU_HEADS):
        gates_ref[:, 2 * h * dh:2 * (h + 1) * dh] = _dot(xcb[:, h * dh:(h + 1) * dh], wab_ref[h])
    _order_after(xn_ref, gates_ref[:, 2 * (LRU_HEADS - 1) * dh:2 * LRU_HEADS * dh], zero_ref)
    rest = proj[-1:] + proj[2 * per_group:-1]
    share = -(-len(rest) // LRU_HEADS)
    for h in range(LRU_HEADS):
        _run(rest[h * share:(h + 1) * share])
        sl = slice(h * dh, (h + 1) * dh)
        gates = gates_ref[:, 2 * h * dh:2 * (h + 1) * dh]
        r = jax.nn.sigmoid(gates[:, 0:dh] + ba_ref[:, sl])
        i = jax.nn.sigmoid(gates[:, dh:2 * dh] + bx_ref[:, sl])
        log_a = -LRU_C * r * sp[:, sl]
        a = jnp.exp(log_a)
        b = jnp.sqrt(-jnp.tanh(log_a) * (a * a + 1.0)) * (i * xc[:, sl])
        hcol, h_ref[:, sl] = _linear_scan(a, b, h_ref[:, sl])
        mixa_ref[:, sl] = (hcol * jax.nn.gelu(py_ref[:, sl])).astype(mixa_ref.dtype)

    glr = pg_ref[:, 0:LANES].astype(BF16)
    gpre = _dot(glr, gup_ref[...]) + gb_ref[...]
    bc_ref[...] = _chunk_cumsum(_chunk_tril(n), _log_sigmoid(gpre) * (1.0 / GLA_TAU))
    crow =lax.broadcasted_iota(jnp.int32, (CHUNK, CHUNK), 0)
    ccol = lax.broadcasted_iota(jnp.int32, (CHUNK, CHUNK), 1)
    causal = ccol <= crow
    n_chunks = n // CHUNK
    ow = wout_ref.shape[1] // n_chunks
    parts = {}
    for c in range(n_chunks):
        ma_ref[:, c * ow:(c + 1) * ow] = _dot(mixa_ref[...], wout_ref[0:d_half, c * ow:(c + 1) * ow])
        rs = slice(c * CHUNK, (c + 1) * CHUNK)
        for h in range(GLA_HEADS):
            ks = slice(h * dk, (h + 1) * dk)
            bcum = bc_ref[rs, ks]
            q = pqk_ref[rs, ks]
            k = pqk_ref[rs, c_k + h * dk:c_k + (h + 1) * dk]
            v = pv_ref[rs, h * dv:(h + 1) * dv].astype(BF16)
            qd = ((q * (dk ** -0.5)) * jnp.exp(bcum)).astype(BF16)
            ki = (k * jnp.exp(-bcum)).astype(BF16)
            s = jnp.where(causal, _dot_nt(qd, ki), 0.0).astype(BF16)
            blast = bcum[CHUNK - 1:CHUNK, :]
            kd = (k * jnp.exp(blast - bcum)).astype(BF16)
            parts[c, h] = (qd, _dot(s, v), jnp.exp(blast), _dot_tn(v, kd))
    for c in range(n_chunks):
        rs = slice(c * CHUNK, (c + 1) * CHUNK)
        for h in range(GLA_HEADS):
            qd, o_intra, decay, kv = parts[c, h]
            st = st_ref[h]
            o = o_intra + _dot_nt(qd, st.astype(BF16))
            st_ref[h] = decay * st + kv
            rg = pr_ref[rs, h * dv:(h + 1) * dv]
            on = _rms(o, gn_ref[:, h * dv:(h + 1) * dv]) * (rg * jax.nn.sigmoid(rg))
            mixb_ref[rs, h * dv:(h + 1) * dv] = on.astype(mixb_ref.dtype)

    m = ma_ref[...] + _dot(mixb_ref[...], wout_ref[d_half:, :])
    o_ref[...] = x + _rms(m, g3_ref[...])


def _full(a):
    return pl.BlockSpec(a.shape, lambda b, t: (0,) * a.ndim)


def _even_layer(x, g2, w_in, cw, cb, wab, ba, bx, lam, gup, gb, gn, w_out, g3):
    bsz, s, d = x.shape
    d_half = cw.shape[1]
    n = min(MIX_TIME_TILE, s)
    assert s % n == 0 and n % CHUNK == 0 and w_in.shape[1] % MXU_DIM == 0
    dh = d_half // LRU_HEADS
    dk = gup.shape[1] // GLA_HEADS
    dv = d_half // GLA_HEADS
    zero = jnp.zeros((SUBLANES, LANES), jnp.int32)
    params = (g2, w_in, cw, cb, wab, ba, bx, lam, gup, gb, gn, w_out, g3, zero)
    tile = pl.BlockSpec((None, n, d), lambda b, t: (b, t, 0))
    return pl.pallas_call(
        functools.partial(_even_body, dh=dh, dk=dk, dv=dv),
        out_shape=jax.ShapeDtypeStruct((bsz, s, d), F32),
        grid=(bsz, s // n),
        in_specs=[tile] + [_full(a) for a in params],
        out_specs=tile,
        scratch_shapes=[
            pltpu.VMEM((n, d_half), F32),
            pltpu.VMEM((n, d_half), F32),
            pltpu.VMEM((n, 2 * GLA_HEADS * dk), F32),
            pltpu.VMEM((n, d_half), F32),
            pltpu.VMEM((n, d_half), F32),
            pltpu.VMEM((n, w_in.shape[1] - 4 * d_half - 2 * GLA_HEADS * dk), F32),
            pltpu.VMEM((n, d_half), BF16),
            pltpu.VMEM((n, d_half), BF16),
            pltpu.VMEM((n + CONV_HIST, d_half), F32),
            pltpu.VMEM((1, d_half), F32),
            pltpu.VMEM((GLA_HEADS, dv, dk), F32),
            pltpu.VMEM((n, GLA_HEADS * dk), F32),
            pltpu.VMEM((n, d), F32),
            pltpu.VMEM((n, d), BF16),
            pltpu.VMEM((n, 2 * d_half), F32),
        ],
        compiler_params=pltpu.CompilerParams(
            dimension_semantics=("parallel", "arbitrary"),
            vmem_limit_bytes=VMEM_LIMIT_BYTES),
        name="even_layer",
    )(x, *params)


def _odd_body(x_ref, g2_ref, win_ref, cw_ref, cb_ref, wq_ref, wk_ref, wv_ref, wif_ref, bif_ref,
              skip_ref, mn_ref, pw_ref, ps_ref, wout_ref, g3_ref, o_ref, pxm_ref, pz_ref, pu_ref,
              hn_ref, mixb_ref, mb_ref, xs_ref, us_ref, c_ref, n_ref, m_ref, xc_ref, q_ref, k_ref,
              v_ref, *, dh):
    n = o_ref.shape[0]
    d_half = MLSTM_HEADS * dh
    t = pl.program_id(1)

    @pl.when(t == 0)
    def _():
        xs_ref[0:CONV_HIST, :] = jnp.zeros((CONV_HIST, d_half), F32)
        us_ref[0:POOL_HIST, :] = jnp.zeros((POOL_HIST, d_half), F32)
        c_ref[...] = jnp.zeros_like(c_ref)
        n_ref[...] = jnp.zeros_like(n_ref)
        m_ref[...] = jnp.zeros_like(m_ref)

    x = x_ref[...]
    xn = _rms(x, g2_ref[...]).astype(BF16)
    proj = _proj_pieces(lambda: xn, win_ref, (pxm_ref, pz_ref, pu_ref), PROJ_PIECE)
    per_group = d_half // PROJ_PIECE
    _run(proj[0:per_group])

    xm = pxm_ref[...]
    xc = _causal_conv(xs_ref, xm, cw_ref[...], cb_ref[...], n)
    _run(proj[2 * per_group:3 * per_group])
    xc = xc * jax.nn.sigmoid(xc)
    xc_ref[...] = xc
    xcb = xc.astype(BF16)
    xmb = xm.astype(BF16)
    u = pu_ref[...]
    us_ref[POOL_HIST:POOL_HIST + n, :] = u
    pos = t * n + lax.broadcasted_iota(jnp.int32, (n, 1), 0)
    gw = d_half // len(POOL_WINDOWS)
    assert len(POOL_WINDOWS) == MLSTM_HEADS
    gates = bif_ref[...]
    for h in range(MLSTM_HEADS):
        cs = slice(h * gw, (h + 1) * gw)
        s = us_ref[:, cs]
        d = 1
        while d < POOL_WINDOWS[h]:
            s = s + pltpu.roll(s, d, axis=0)
            d *= 2
        cnt = jnp.minimum(pos + 1, POOL_WINDOWS[h]).astype(F32)
        pooled = s[POOL_HIST:POOL_HIST + n, :] / cnt - u[:, cs]
        y = _dot(pooled.astype(BF16), pw_ref[h]) * ps_ref[:, cs]
        mixb_ref[:, cs] = y.astype(mixb_ref.dtype)

        sl = slice(h * dh, (h + 1) * dh)
        q = _dot(xcb[:, sl], wq_ref[h])
        k = _dot(xcb[:, sl], wk_ref[h])
        v = _dot(xmb[:, sl], wv_ref[h])
        q_ref[:, sl] = q
        k_ref[:, sl] = k
        v_ref[:, sl] = v
        gates = (gates + _dot(q.astype(BF16), wif_ref[0, sl, :])
                 + _dot(k.astype(BF16), wif_ref[1, sl, :])
                 + _dot(v.astype(BF16), wif_ref[2, sl, :]))
    bcum_all = _chunk_cumsum(_chunk_tril(n), _log_sigmoid(gates))
    gates_t = gates.T
    bcum_t = bcum_all.T
    us_ref[0:POOL_HIST, :] = us_ref[n:n + POOL_HIST, :]

    crow = lax.broadcasted_iota(jnp.int32, (CHUNK, CHUNK), 0)
    ccol = lax.broadcasted_iota(jnp.int32, (CHUNK, CHUNK), 1)
    causal = ccol <= crow
    n_chunks = n // CHUNK

    def gate_views(c, h):
        rs = slice(c * CHUNK, (c + 1) * CHUNK)
        fcol = MLSTM_HEADS + h
        b_c = bcum_all[rs, fcol:fcol + 1]
        b_r = bcum_t[fcol:fcol + 1, rs]
        i_c = gates[rs, h:h + 1]
        i_r = gates_t[h:h + 1, rs]
        return b_c, b_r, i_c, i_r

    m_at = {}
    for h in range(MLSTM_HEADS):
        m_run = m_ref[h]
        for c in range(n_chunks):
            b_c, b_r, _, i_r = gate_views(c, h)
            b_l = b_c[CHUNK - 1:CHUNK, :]
            m_at[c, h] = m_run
            m_run = jnp.maximum(b_l + m_run, jnp.max(b_l - b_r + i_r, axis=-1, keepdims=True))
        m_at[n_chunks, h] = m_run
        m_ref[h] = m_run

    zpieces = proj[per_group:2 * per_group]
    zshare = -(-len(zpieces) // n_chunks)
    ow = wout_ref.shape[1] // n_chunks
    parts = {}
    for c in range(n_chunks):
        _run(zpieces[c * zshare:(c + 1) * zshare])
        mb_ref[:, c * ow:(c + 1) * ow] = _dot(mixb_ref[...], wout_ref[d_half:, c * ow:(c + 1) * ow])
        rs = slice(c * CHUNK, (c + 1) * CHUNK)
        for h in range(MLSTM_HEADS):
            sl = slice(h * dh, (h + 1) * dh)
            b_c, b_r, i_c, i_r = gate_views(c, h)
            m_prev, m_new = m_at[c, h], m_at[c + 1, h]
            dmat = jnp.where(causal, b_c - b_r + i_r, -jnp.inf)
            inter = b_c + m_prev
            m_t = jnp.maximum(inter, jnp.max(dmat, axis=-1, keepdims=True))
            w = jnp.exp(dmat - m_t)
            inter_w = jnp.exp(inter - m_t)
            qb = q_ref[rs, sl].astype(BF16)
            k = k_ref[rs, sl] * (dh ** -0.5)
            kb = k.astype(BF16)
            v = v_ref[rs, sl]
            qk = _dot_nt(qb, kb) * w
            b_l = b_c[CHUNK - 1:CHUNK, :]
            decay = jnp.exp(b_l + m_prev - m_new)
            w_c = jnp.exp(b_l - b_c + i_c - m_new)
            parts[c, h] = (
                qb, _dot(qk.astype(BF16), v.astype(BF16)), jnp.sum(qk, axis=-1, keepdims=True),
                inter_w, jnp.exp(-m_t), decay, _dot_tn((v * w_c).astype(BF16), kb),
                jnp.sum(k * w_c, axis=0, keepdims=True))
    for c in range(n_chunks):
        rs = slice(c * CHUNK, (c + 1) * CHUNK)
        for h in range(MLSTM_HEADS):
            sl = slice(h * dh, (h + 1) * dh)
            qb, num_intra, den_intra, inter_w, floor, decay, kvw, ksum = parts[c, h]
            cst = c_ref[h]
            nst = n_ref[h]
            num = num_intra + inter_w * _dot_nt(qb, cst.astype(BF16))
            den = den_intra + inter_w * jnp.sum(q_ref[rs, sl] * nst, axis=-1, keepdims=True)
            hid = num / jnp.maximum(jnp.abs(den), floor)
            c_ref[h] = decay * cst + kvw
            n_ref[h] = decay * nst + ksum
            hn_ref[rs, sl] = _rms(hid, mn_ref[:, sl])

    m = mb_ref[...]
    for h in range(MLSTM_HEADS):
        sl = slice(h * dh, (h + 1) * dh)
        out = jax.nn.sigmoid(pz_ref[:, sl]) * (hn_ref[:, sl] + skip_ref[:, sl] * xc_ref[:, sl])
        m = m + _dot(out.astype(BF16), wout_ref[sl, :])
    o_ref[...] = x + _rms(m, g3_ref[...])


def _odd_layer(x, g2, w_in, cw, cb, wq, wk, wv, wif, bif, skip, mn, pw, ps, w_out, g3):
    bsz, s, d = x.shape
    d_half = cw.shape[1]
    n = min(MIX_TIME_TILE, s)
    assert s % n == 0 and n % CHUNK == 0 and w_in.shape[1] == 3 * d_half
    dh = d_half // MLSTM_HEADS
    params = (g2, w_in, cw, cb, wq, wk, wv, wif, bif, skip, mn, pw, ps, w_out, g3)
    tile = pl.BlockSpec((None, n, d), lambda b, t: (b, t, 0))
    return pl.pallas_call(
        functools.partial(_odd_body, dh=dh),
        out_shape=jax.ShapeDtypeStruct((bsz, s, d), F32),
        grid=(bsz, s // n),
        in_specs=[tile] + [_full(a) for a in params],
        out_specs=tile,
        scratch_shapes=[
            pltpu.VMEM((n, d_half), F32),
            pltpu.VMEM((n, d_half), F32),
            pltpu.VMEM((n, d_half), F32),
            pltpu.VMEM((n, d_half), F32),
            pltpu.VMEM((n, d_half), BF16),
            pltpu.VMEM((n, d), F32),
            pltpu.VMEM((n + CONV_HIST, d_half), F32),
            pltpu.VMEM((n + POOL_HIST, d_half), F32),
            pltpu.VMEM((MLSTM_HEADS, dh, dh), F32),
            pltpu.VMEM((MLSTM_HEADS, 1, dh), F32),
            pltpu.VMEM((MLSTM_HEADS, 1, 1), F32),
            pltpu.VMEM((n, d_half), F32),
            pltpu.VMEM((n, d_half), F32),
            pltpu.VMEM((n, d_half), F32),
            pltpu.VMEM((n, d_half), F32),
        ],
        compiler_params=pltpu.CompilerParams(
            dimension_semantics=("parallel", "arbitrary"),
            vmem_limit_bytes=VMEM_LIMIT_BYTES),
        name="odd_layer",
    )(x, *params)


def _pad_cols(w, mult):
    pad = (-w.shape[-1]) % mult
    return jnp.pad(w, [(0, 0)] * (w.ndim - 1) + [(0, pad)])


def _blockdiag_dense(w, heads):
    nb, bs, _ = w.shape
    size = nb * bs
    per = size // heads
    tiled = jnp.tile(w.reshape(size, bs), (1, per // bs))
    row = lax.broadcasted_iota(jnp.int32, (size, per), 0)
    col = lax.broadcasted_iota(jnp.int32, (size, per), 1)
    dense = jnp.where((row % per) // bs == col // bs, tiled, 0)
    return dense.reshape(heads, per, per)


def kernel(x, norm_g, ffn_wg, ffn_wu, ffn_wd, ev_w_in, ev_w_out, lru_conv_w, lru_conv_b, lru_wa, lru_ba, lru_wx, lru_bx, lru_lambda, gla_w_gate, gla_b_gate, gla_norm, od_w_in, od_w_out, mlstm_conv_w, mlstm_conv_b, mlstm_wq, mlstm_wk, mlstm_wv, mlstm_w_if, mlstm_b_if, mlstm_skip, mlstm_norm, pool_w, pool_scale):
    bsz, seq, d = x.shape
    depth = norm_g.shape[0]
    d_half = lru_conv_w.shape[-1]
    t = bsz * seq
    row = lambda v: v.reshape(1, -1).astype(F32)
    wg16, wu16, wd16 = ffn_wg.astype(BF16), ffn_wu.astype(BF16), ffn_wd.astype(BF16)

    def ffn(xs, layer, which, g_pre, g_post):
        y = _ffn(xs.reshape(t, d), row(g_pre), wg16, wu16, wd16, row(0.5 * g_post), layer, which)
        return y.reshape(bsz, seq, d)

    for layer in range(depth):
        g = norm_g[layer]
        j = layer // 2
        x = ffn(x, layer, 0, g[0], g[1])
        if layer % 2 == 0:
            w_in = _pad_cols(ev_w_in[j], MXU_DIM).astype(BF16)
            wab = jnp.concatenate([lru_wa[j], lru_wx[j]], axis=-1).astype(BF16)
            rank = gla_w_gate.shape[1]
            gup = jnp.pad(gla_w_gate[j], ((0, LANES - rank), (0, 0))).astype(BF16)
            x = _even_layer(
                x, row(g[2]), w_in, lru_conv_w[j], row(lru_conv_b[j]), wab, row(lru_ba[j]),
                row(lru_bx[j]), row(lru_lambda[j]), gup, row(gla_b_gate[j]), row(gla_norm[j]),
                ev_w_out[j].astype(BF16), row(g[3]))
        else:
            wq = _blockdiag_dense(mlstm_wq[j], MLSTM_HEADS).astype(BF16)
            wk = _blockdiag_dense(mlstm_wk[j], MLSTM_HEADS).astype(BF16)
            wv = _blockdiag_dense(mlstm_wv[j], MLSTM_HEADS).astype(BF16)
            wif = _pad_cols(mlstm_w_if[j], LANES).reshape(3, d_half, LANES).astype(BF16)
            bif = _pad_cols(mlstm_b_if[j].reshape(1, -1), LANES).astype(F32)
            x = _odd_layer(
                x, row(g[2]), od_w_in[j].astype(BF16), mlstm_conv_w[j], row(mlstm_conv_b[j]),
                wq, wk, wv, wif, bif, row(mlstm_skip[j]), row(mlstm_norm[j]),
                pool_w[j].astype(BF16), row(pool_scale[j]), od_w_out[j].astype(BF16), row(g[3]))
        x = ffn(x, layer, 1, g[4], g[5])
    return x
```

```python
import functools

import jax
import jax.numpy as jnp
from jax import lax
from jax.experimental import pallas as pl
from jax.experimental.pallas import tpu as pltpu

F32 = jnp.float32
BF16 = jnp.bfloat16

EPS = 1e-6
LANES = 128
SUBLANES = 8
MXU_DIM = 256
VMEM_LIMIT_BYTES = 56 * 1024 * 1024

CONV_W = 4
LRU_HEADS = 8
LRU_C = 8.0
GLA_HEADS = 4
GLA_TAU = 16.0
MLSTM_HEADS = 4
POOL_WINDOWS = (2, 4, 8, 16)
CHUNK = 64

FFN_ROW_TILE = 1024
FFN_HID_TILE = 512
FFN_EDGE_ROWS = 512
MIX_TIME_TILE = 256
PROJ_PIECE = MXU_DIM
CONV_HIST = SUBLANES
POOL_HIST = 16


def _rms(y, g):
    return y * lax.rsqrt(jnp.mean(y * y, axis=-1, keepdims=True) + EPS) * g


def _softplus(x):
    return jnp.maximum(x, 0.0) + jnp.log1p(jnp.exp(-jnp.abs(x)))


def _log_sigmoid(x):
    return -_softplus(-x)


def _dot(a, b):
    return jnp.dot(a, b, preferred_element_type=F32)


def _dot_nt(a, b):
    return lax.dot_general(a, b, (((1,), (1,)), ((), ())), preferred_element_type=F32)


def _dot_tn(a, b):
    return lax.dot_general(a, b, (((0,), (0,)), ((), ())), preferred_element_type=F32)


def _chunk_tril(n):
    row = lax.broadcasted_iota(jnp.int32, (n, n), 0)
    col = lax.broadcasted_iota(jnp.int32, (n, n), 1)
    same = (row // CHUNK) == (col // CHUNK)
    return jnp.where(same & (col <= row), 1.0, 0.0).astype(BF16)


def _chunk_cumsum(tril, x):
    x1 = x.astype(BF16)
    r1 = x - x1.astype(F32)
    x2 = r1.astype(BF16)
    x3 = (r1 - x2.astype(F32)).astype(BF16)
    return _dot(tril, x1) + _dot(tril, x2) + _dot(tril, x3)


def _proj_pieces(lhs, w_ref, p_refs, width):
    pieces = []
    c0 = 0
    for p_ref in p_refs:
        for l0 in range(0, p_ref.shape[1], width):
            def piece(p_ref=p_ref, l0=l0, c=c0 + l0, w=min(width, p_ref.shape[1] - l0)):
                p_ref[:, l0:l0 + w] = _dot(lhs(), w_ref[:, c:c + w])
            pieces.append(piece)
        c0 += p_ref.shape[1]
    assert c0 == w_ref.shape[1]
    return pieces


def _run(pieces):
    for piece in pieces:
        piece()


def _order_after(dst_ref, value, zero):
    rows = SUBLANES * (4 // dst_ref.dtype.itemsize)
    bits = pltpu.bitcast(dst_ref[0:rows, 0:LANES], jnp.int32)
    dep = pltpu.bitcast(value[0:SUBLANES, 0:LANES], jnp.int32) & zero
    dst_ref[0:rows, 0:LANES] = pltpu.bitcast(bits | dep, dst_ref.dtype)


class _Rows:
    def __init__(self, ref, r0, width, count=1):
        self.ref, self.r0, self.width, self.count = ref, r0, width, count

    def __getitem__(self, idx):
        rows = slice(self.r0, self.r0 + self.count)
        if idx is Ellipsis:
            return self.ref[rows, 0:self.width]
        lead, cols = idx
        assert lead == slice(None) and cols.stop <= self.width
        return self.ref[rows, cols]


def _pack_rows(vectors, width, n_rows):
    rows = [jnp.pad(v.reshape(-1, v.shape[-1]).astype(F32), ((0, 0), (0, width - v.shape[-1])))
            for v in vectors]
    packed = jnp.concatenate(rows, axis=0)
    return jnp.pad(packed, ((0, n_rows - packed.shape[0]), (0, 0)))


def _causal_conv(xs_ref, x, cw, cb, n):
    xs_ref[CONV_HIST:CONV_HIST + n, :] = x
    acc = cb
    for k in range(CONV_W):
        off = CONV_HIST - (CONV_W - 1) + k
        acc = acc + cw[k:k + 1, :] * xs_ref[off:off + n, :]
    xs_ref[0:CONV_HIST, :] = xs_ref[n:n + CONV_HIST, :]
    return acc


def _linear_scan(a, b, carry):
    n, c = a.shape
    groups = n // SUBLANES
    a3 = a.reshape(groups, SUBLANES, c)
    b3 = b.reshape(groups, SUBLANES, c)
    sub = lax.broadcasted_iota(jnp.int32, (groups, SUBLANES, c), 1)
    d = 1
    while d < SUBLANES:
        a_sh = jnp.where(sub >= d, pltpu.roll(a3, d, axis=1), 1.0)
        b_sh = jnp.where(sub >= d, pltpu.roll(b3, d, axis=1), 0.0)
        b3 = a3 * b_sh + b3
        a3 = a3 * a_sh
        d *= 2
    outs = []
    for v in range(groups):
        hv = b3[v] + a3[v] * carry
        outs.append(hv)
        carry = hv[SUBLANES - 1:SUBLANES, :]
    return jnp.concatenate(outs, axis=0), carry


def _ffn_step(x_ref, gains_ref, wg_ref, wu_ref, wd_ref, o_ref, xn_ref, *, first, final, bounds):
    acc_ref = o_ref
    for r0, r1 in zip(bounds[:-1], bounds[1:]):
        rs = slice(r0, r1)
        if first:
            xn_ref[rs, :] = _rms(x_ref[rs, :], gains_ref[0:1, :]).astype(BF16)
        xn = xn_ref[rs, :]
        a = _dot(xn, wg_ref[...])
        b = _dot(xn, wu_ref[...])
        h = (a * jax.nn.sigmoid(a) * b).astype(BF16)
        upd = _dot(h, wd_ref[...])
        acc = upd if first else acc_ref[rs, :] + upd
        if final:
            o_ref[rs, :] = x_ref[rs, :] + _rms(acc, gains_ref[1:2, :])
        else:
            acc_ref[rs, :] = acc


def _ffn_body(*refs):
    j = pl.program_id(1)
    last = pl.num_programs(1) - 1
    tm = refs[0].shape[0]
    pl.when(j == 0)(functools.partial(_ffn_step, *refs, first=True, final=False,
                                      bounds=(0, FFN_EDGE_ROWS, tm)))
    pl.when(jnp.logical_and(j > 0, j < last))(
        functools.partial(_ffn_step, *refs, first=False, final=False, bounds=(0, tm)))
    pl.when(j == last)(functools.partial(_ffn_step, *refs, first=False, final=True,
                                         bounds=(0, tm - FFN_EDGE_ROWS, tm)))


def _ffn(x, gains, wg, wu, wd, layer, which):
    t, d = x.shape
    ff = wg.shape[-1]
    tm, tf = min(FFN_ROW_TILE, t), FFN_HID_TILE
    assert t % tm == 0 and ff % tf == 0 and ff // tf >= 2 and tm > FFN_EDGE_ROWS
    return pl.pallas_call(
        _ffn_body,
        out_shape=jax.ShapeDtypeStruct((t, d), F32),
        grid=(t // tm, ff // tf),
        in_specs=[
            pl.BlockSpec((tm, d), lambda i, j: (i, 0)),
            pl.BlockSpec((2, d), lambda i, j: (0, 0)),
            pl.BlockSpec((None, None, d, tf), lambda i, j: (layer, which, 0, j)),
            pl.BlockSpec((None, None, d, tf), lambda i, j: (layer, which, 0, j)),
            pl.BlockSpec((None, None, tf, d), lambda i, j: (layer, which, j, 0)),
        ],
        out_specs=pl.BlockSpec((tm, d), lambda i, j: (i, 0)),
        scratch_shapes=[pltpu.VMEM((tm, d), BF16)],
        compiler_params=pltpu.CompilerParams(
            dimension_semantics=("parallel", "arbitrary"),
            vmem_limit_bytes=VMEM_LIMIT_BYTES),
        name="ffn",
    )(x, gains, wg, wu, wd)


EVEN_ROWS = 24


def _even_body(x_ref, rows_ref, win_ref, wab_ref, gup_ref, wout_ref, o_ref, px_ref, py_ref,
               pqk_ref, pv_ref, pr_ref, pg_ref, mixa_ref, mixb_ref, xs_ref, h_ref, st_ref, bc_ref,
               ma_ref, xn_ref, gates_ref, *, dh, dk, dv):
    n = o_ref.shape[0]
    d = o_ref.shape[1]
    d_half = LRU_HEADS * dh
    g2_ref, g3_ref = _Rows(rows_ref, 0, d), _Rows(rows_ref, 1, d)
    cb_ref, ba_ref, bx_ref, lam_ref = (_Rows(rows_ref, r, d_half) for r in (2, 3, 4, 5))
    gb_ref, gn_ref = _Rows(rows_ref, 6, GLA_HEADS * dk), _Rows(rows_ref, 7, d_half)
    cw_ref = _Rows(rows_ref, 8, d_half, CONV_W)
    zero = pltpu.bitcast(rows_ref[16:16 + SUBLANES, 0:LANES], jnp.int32)

    @pl.when(pl.program_id(1) == 0)
    def _():
        xs_ref[0:CONV_HIST, :] = jnp.zeros((CONV_HIST, d_half), F32)
        h_ref[...] = jnp.zeros_like(h_ref)
        st_ref[...] = jnp.zeros_like(st_ref)

    x = x_ref[...]
    xn_ref[...] = _rms(x, g2_ref[...]).astype(BF16)
    proj = _proj_pieces(lambda: xn_ref[...], win_ref,
                        (px_ref, py_ref, pqk_ref, pv_ref, pr_ref, pg_ref), PROJ_PIECE)
    per_group = d_half // PROJ_PIECE
    _run(proj[0:per_group])
    c_k = GLA_HEADS * dk

    xc = _causal_conv(xs_ref, px_ref[...], cw_ref[...], cb_ref[...], n)
    _run(proj[per_group:2 * per_group])
    xcb = xc.astype(BF16)
    sp = _softplus(-lam_ref[...])
    for h in range(LRU_HEADS):
        gates_ref[:, 2 * h * dh:2 * (h + 1) * dh] = _dot(xcb[:, h * dh:(h + 1) * dh], wab_ref[h])
    _order_after(xn_ref, gates_ref[:, 2 * (LRU_HEADS - 1) * dh:2 * LRU_HEADS * dh], zero)
    rest = proj[-1:] + proj[2 * per_group:-1]
    share = -(-len(rest) // LRU_HEADS)
    for h in range(LRU_HEADS):
        _run(rest[h * share:(h + 1) * share])
        sl = slice(h * dh, (h + 1) * dh)
        gates = gates_ref[:, 2 * h * dh:2 * (h + 1) * dh]
        r = jax.nn.sigmoid(gates[:, 0:dh] + ba_ref[:, sl])
        i = jax.nn.sigmoid(gates[:, dh:2 * dh] + bx_ref[:, sl])
        log_a = -LRU_C * r * sp[:, sl]
        a = jnp.exp(log_a)
        b = jnp.sqrt(-jnp.tanh(log_a) * (a * a + 1.0)) * (i * xc[:, sl])
        hcol, h_ref[:, sl] = _linear_scan(a, b, h_ref[:, sl])
        mixa_ref[:, sl] = (hcol * jax.nn.gelu(py_ref[:, sl])).astype(mixa_ref.dtype)

    glr = pg_ref[:, 0:LANES].astype(BF16)
    gpre = _dot(glr, gup_ref[...]) + gb_ref[...]
    bc_ref[...] = _chunk_cumsum(_chunk_tril(n), _log_sigmoid(gpre) * (1.0 / GLA_TAU))
    crow =lax.broadcasted_iota(jnp.int32, (CHUNK, CHUNK), 0)
    ccol = lax.broadcasted_iota(jnp.int32, (CHUNK, CHUNK), 1)
    causal = ccol <= crow
    n_chunks = n // CHUNK
    ow = wout_ref.shape[1] // n_chunks
    parts = {}
    for c in range(n_chunks):
        ma_ref[:, c * ow:(c + 1) * ow] = _dot(mixa_ref[...], wout_ref[0:d_half, c * ow:(c + 1) * ow])
        rs = slice(c * CHUNK, (c + 1) * CHUNK)
        for h in range(GLA_HEADS):
            ks = slice(h * dk, (h + 1) * dk)
            bcum = bc_ref[rs, ks]
            q = pqk_ref[rs, ks]
            k = pqk_ref[rs, c_k + h * dk:c_k + (h + 1) * dk]
            v = pv_ref[rs, h * dv:(h + 1) * dv].astype(BF16)
            qd = ((q * (dk ** -0.5)) * jnp.exp(bcum)).astype(BF16)
            ki = (k * jnp.exp(-bcum)).astype(BF16)
            s = jnp.where(causal, _dot_nt(qd, ki), 0.0).astype(BF16)
            blast = bcum[CHUNK - 1:CHUNK, :]
            kd = (k * jnp.exp(blast - bcum)).astype(BF16)
            parts[c, h] = (qd, _dot(s, v), jnp.exp(blast), _dot_tn(v, kd))
    for c in range(n_chunks):
        rs = slice(c * CHUNK, (c + 1) * CHUNK)
        for h in range(GLA_HEADS):
            qd, o_intra, decay, kv = parts[c, h]
            st = st_ref[h]
            o = o_intra + _dot_nt(qd, st.astype(BF16))
            st_ref[h] = decay * st + kv
            rg = pr_ref[rs, h * dv:(h + 1) * dv]
            on = _rms(o, gn_ref[:, h * dv:(h + 1) * dv]) * (rg * jax.nn.sigmoid(rg))
            mixb_ref[rs, h * dv:(h + 1) * dv] = on.astype(mixb_ref.dtype)

    m = ma_ref[...] + _dot(mixb_ref[...], wout_ref[d_half:, :])
    o_ref[...] = x + _rms(m, g3_ref[...])


def _full(a):
    return pl.BlockSpec(a.shape, lambda b, t: (0,) * a.ndim)


def _even_layer(x, g2, w_in, cw, cb, wab, ba, bx, lam, gup, gb, gn, w_out, g3):
    bsz, s, d = x.shape
    d_half = cw.shape[1]
    n = min(MIX_TIME_TILE, s)
    assert s % n == 0 and n % CHUNK == 0 and w_in.shape[1] % MXU_DIM == 0
    dh = d_half // LRU_HEADS
    dk = gup.shape[1] // GLA_HEADS
    dv = d_half // GLA_HEADS
    rows = _pack_rows((g2, g3, cb, ba, bx, lam, gb, gn, cw), d, EVEN_ROWS)
    params = (rows, w_in, wab, gup, w_out)
    tile = pl.BlockSpec((None, n, d), lambda b, t: (b, t, 0))
    return pl.pallas_call(
        functools.partial(_even_body, dh=dh, dk=dk, dv=dv),
        out_shape=jax.ShapeDtypeStruct((bsz, s, d), F32),
        grid=(bsz, s // n),
        in_specs=[tile] + [_full(a) for a in params],
        out_specs=tile,
        scratch_shapes=[
            pltpu.VMEM((n, d_half), F32),
            pltpu.VMEM((n, d_half), F32),
            pltpu.VMEM((n, 2 * GLA_HEADS * dk), F32),
            pltpu.VMEM((n, d_half), F32),
            pltpu.VMEM((n, d_half), F32),
            pltpu.VMEM((n, w_in.shape[1] - 4 * d_half - 2 * GLA_HEADS * dk), F32),
            pltpu.VMEM((n, d_half), BF16),
            pltpu.VMEM((n, d_half), BF16),
            pltpu.VMEM((n + CONV_HIST, d_half), F32),
            pltpu.VMEM((1, d_half), F32),
            pltpu.VMEM((GLA_HEADS, dv, dk), F32),
            pltpu.VMEM((n, GLA_HEADS * dk), F32),
            pltpu.VMEM((n, d), F32),
            pltpu.VMEM((n, d), BF16),
            pltpu.VMEM((n, 2 * d_half), F32),
        ],
        compiler_params=pltpu.CompilerParams(
            dimension_semantics=("parallel", "arbitrary"),
            vmem_limit_bytes=VMEM_LIMIT_BYTES),
        name="even_layer",
    )(x, *params)


ODD_ROWS = 16


def _odd_body(x_ref, rows_ref, win_ref, wmat_ref, wif_ref, wout_ref, o_ref, pxm_ref, pz_ref,
              pu_ref, hn_ref, mixb_ref, mb_ref, xs_ref, us_ref, c_ref, n_ref, m_ref, xc_ref, q_ref,
              k_ref, v_ref, *, dh):
    n = o_ref.shape[0]
    d = o_ref.shape[1]
    d_half = MLSTM_HEADS * dh
    t = pl.program_id(1)
    g2_ref, g3_ref = _Rows(rows_ref, 0, d), _Rows(rows_ref, 1, d)
    cb_ref, bif_ref = _Rows(rows_ref, 2, d_half), _Rows(rows_ref, 3, LANES)
    skip_ref, mn_ref, ps_ref = (_Rows(rows_ref, r, d_half) for r in (4, 5, 6))
    cw_ref = _Rows(rows_ref, 7, d_half, CONV_W)
    i_q, i_k, i_v, i_pool = (g * MLSTM_HEADS for g in range(4))

    @pl.when(t == 0)
    def _():
        xs_ref[0:CONV_HIST, :] = jnp.zeros((CONV_HIST, d_half), F32)
        us_ref[0:POOL_HIST, :] = jnp.zeros((POOL_HIST, d_half), F32)
        c_ref[...] = jnp.zeros_like(c_ref)
        n_ref[...] = jnp.zeros_like(n_ref)
        m_ref[...] = jnp.zeros_like(m_ref)

    x = x_ref[...]
    xn = _rms(x, g2_ref[...]).astype(BF16)
    proj = _proj_pieces(lambda: xn, win_ref, (pxm_ref, pz_ref, pu_ref), PROJ_PIECE)
    per_group = d_half // PROJ_PIECE
    _run(proj[0:per_group])

    xm = pxm_ref[...]
    xc = _causal_conv(xs_ref, xm, cw_ref[...], cb_ref[...], n)
    _run(proj[2 * per_group:3 * per_group])
    xc = xc * jax.nn.sigmoid(xc)
    xc_ref[...] = xc
    xcb = xc.astype(BF16)
    xmb = xm.astype(BF16)
    u = pu_ref[...]
    us_ref[POOL_HIST:POOL_HIST + n, :] = u
    pos = t * n + lax.broadcasted_iota(jnp.int32, (n, 1), 0)
    gw = d_half // len(POOL_WINDOWS)
    assert len(POOL_WINDOWS) == MLSTM_HEADS
    gates = bif_ref[...]
    for h in range(MLSTM_HEADS):
        cs = slice(h * gw, (h + 1) * gw)
        s = us_ref[:, cs]
        d = 1
        while d < POOL_WINDOWS[h]:
            s = s + pltpu.roll(s, d, axis=0)
            d *= 2
        cnt = jnp.minimum(pos + 1, POOL_WINDOWS[h]).astype(F32)
        pooled = s[POOL_HIST:POOL_HIST + n, :] / cnt - u[:, cs]
        y = _dot(pooled.astype(BF16), wmat_ref[i_pool + h]) * ps_ref[:, cs]
        mixb_ref[:, cs] = y.astype(mixb_ref.dtype)

        sl = slice(h * dh, (h + 1) * dh)
        q = _dot(xcb[:, sl], wmat_ref[i_q + h])
        k = _dot(xcb[:, sl], wmat_ref[i_k + h])
        v = _dot(xmb[:, sl], wmat_ref[i_v + h])
        q_ref[:, sl] = q
        k_ref[:, sl] = k
        v_ref[:, sl] = v
        gates = (gates + _dot(q.astype(BF16), wif_ref[0, sl, :])
                 + _dot(k.astype(BF16), wif_ref[1, sl, :])
                 + _dot(v.astype(BF16), wif_ref[2, sl, :]))
    bcum_all = _chunk_cumsum(_chunk_tril(n), _log_sigmoid(gates))
    gates_t = gates.T
    bcum_t = bcum_all.T
    us_ref[0:POOL_HIST, :] = us_ref[n:n + POOL_HIST, :]

    crow = lax.broadcasted_iota(jnp.int32, (CHUNK, CHUNK), 0)
    ccol = lax.broadcasted_iota(jnp.int32, (CHUNK, CHUNK), 1)
    causal = ccol <= crow
    n_chunks = n // CHUNK

    def gate_views(c, h):
        rs = slice(c * CHUNK, (c + 1) * CHUNK)
        fcol = MLSTM_HEADS + h
        b_c = bcum_all[rs, fcol:fcol + 1]
        b_r = bcum_t[fcol:fcol + 1, rs]
        i_c = gates[rs, h:h + 1]
        i_r = gates_t[h:h + 1, rs]
        return b_c, b_r, i_c, i_r

    m_at = {}
    for h in range(MLSTM_HEADS):
        m_run = m_ref[h]
        for c in range(n_chunks):
            b_c, b_r, _, i_r = gate_views(c, h)
            b_l = b_c[CHUNK - 1:CHUNK, :]
            m_at[c, h] = m_run
            m_run = jnp.maximum(b_l + m_run, jnp.max(b_l - b_r + i_r, axis=-1, keepdims=True))
        m_at[n_chunks, h] = m_run
        m_ref[h] = m_run

    zpieces = proj[per_group:2 * per_group]
    zshare = -(-len(zpieces) // n_chunks)
    ow = wout_ref.shape[1] // n_chunks
    parts = {}
    for c in range(n_chunks):
        _run(zpieces[c * zshare:(c + 1) * zshare])
        mb_ref[:, c * ow:(c + 1) * ow] = _dot(mixb_ref[...], wout_ref[d_half:, c * ow:(c + 1) * ow])
        rs = slice(c * CHUNK, (c + 1) * CHUNK)
        for h in range(MLSTM_HEADS):
            sl = slice(h * dh, (h + 1) * dh)
            b_c, b_r, i_c, i_r = gate_views(c, h)
            m_prev, m_new = m_at[c, h], m_at[c + 1, h]
            dmat = jnp.where(causal, b_c - b_r + i_r, -jnp.inf)
            inter = b_c + m_prev
            m_t = jnp.maximum(inter, jnp.max(dmat, axis=-1, keepdims=True))
            w = jnp.exp(dmat - m_t)
            inter_w = jnp.exp(inter - m_t)
            qb = q_ref[rs, sl].astype(BF16)
            k = k_ref[rs, sl] * (dh ** -0.5)
            kb = k.astype(BF16)
            v = v_ref[rs, sl]
            qk = _dot_nt(qb, kb) * w
            b_l = b_c[CHUNK - 1:CHUNK, :]
            decay = jnp.exp(b_l + m_prev - m_new)
            w_c = jnp.exp(b_l - b_c + i_c - m_new)
            parts[c, h] = (
                qb, _dot(qk.astype(BF16), v.astype(BF16)), jnp.sum(qk, axis=-1, keepdims=True),
                inter_w, jnp.exp(-m_t), decay, _dot_tn((v * w_c).astype(BF16), kb),
                jnp.sum(k * w_c, axis=0, keepdims=True))
    for c in range(n_chunks):
        rs = slice(c * CHUNK, (c + 1) * CHUNK)
        for h in range(MLSTM_HEADS):
            sl = slice(h * dh, (h + 1) * dh)
            qb, num_intra, den_intra, inter_w, floor, decay, kvw, ksum = parts[c, h]
            cst = c_ref[h]
            nst = n_ref[h]
            num = num_intra + inter_w * _dot_nt(qb, cst.astype(BF16))
            den = den_intra + inter_w * jnp.sum(q_ref[rs, sl] * nst, axis=-1, keepdims=True)
            hid = num / jnp.maximum(jnp.abs(den), floor)
            c_ref[h] = decay * cst + kvw
            n_ref[h] = decay * nst + ksum
            hn_ref[rs, sl] = _rms(hid, mn_ref[:, sl])

    m = mb_ref[...]
    for h in range(MLSTM_HEADS):
        sl = slice(h * dh, (h + 1) * dh)
        out = jax.nn.sigmoid(pz_ref[:, sl]) * (hn_ref[:, sl] + skip_ref[:, sl] * xc_ref[:, sl])
        m = m + _dot(out.astype(BF16), wout_ref[sl, :])
    o_ref[...] = x + _rms(m, g3_ref[...])


def _odd_layer(x, g2, w_in, cw, cb, wq, wk, wv, wif, bif, skip, mn, pw, ps, w_out, g3):
    bsz, s, d = x.shape
    d_half = cw.shape[1]
    n = min(MIX_TIME_TILE, s)
    assert s % n == 0 and n % CHUNK == 0 and w_in.shape[1] == 3 * d_half
    dh = d_half // MLSTM_HEADS
    assert pw.shape == wq.shape == (MLSTM_HEADS, dh, dh)
    rows = _pack_rows((g2, g3, cb, bif, skip, mn, ps, cw), d, ODD_ROWS)
    params = (rows, w_in, jnp.concatenate([wq, wk, wv, pw], axis=0), wif, w_out)
    tile = pl.BlockSpec((None, n, d), lambda b, t: (b, t, 0))
    return pl.pallas_call(
        functools.partial(_odd_body, dh=dh),
        out_shape=jax.ShapeDtypeStruct((bsz, s, d), F32),
        grid=(bsz, s // n),
        in_specs=[tile] + [_full(a) for a in params],
        out_specs=tile,
        scratch_shapes=[
            pltpu.VMEM((n, d_half), F32),
            pltpu.VMEM((n, d_half), F32),
            pltpu.VMEM((n, d_half), F32),
            pltpu.VMEM((n, d_half), F32),
            pltpu.VMEM((n, d_half), BF16),
            pltpu.VMEM((n, d), F32),
            pltpu.VMEM((n + CONV_HIST, d_half), F32),
            pltpu.VMEM((n + POOL_HIST, d_half), F32),
            pltpu.VMEM((MLSTM_HEADS, dh, dh), F32),
            pltpu.VMEM((MLSTM_HEADS, 1, dh), F32),
            pltpu.VMEM((MLSTM_HEADS, 1, 1), F32),
            pltpu.VMEM((n, d_half), F32),
            pltpu.VMEM((n, d_half), F32),
            pltpu.VMEM((n, d_half), F32),
            pltpu.VMEM((n, d_half), F32),
        ],
        compiler_params=pltpu.CompilerParams(
            dimension_semantics=("parallel", "arbitrary"),
            vmem_limit_bytes=VMEM_LIMIT_BYTES),
        name="odd_layer",
    )(x, *params)


def _pad_cols(w, mult):
    pad = (-w.shape[-1]) % mult
    return jnp.pad(w, [(0, 0)] * (w.ndim - 1) + [(0, pad)])


def _blockdiag_dense(w, heads):
    nb, bs, _ = w.shape
    size = nb * bs
    per = size // heads
    tiled = jnp.tile(w.reshape(size, bs), (1, per // bs))
    row = lax.broadcasted_iota(jnp.int32, (size, per), 0)
    col = lax.broadcasted_iota(jnp.int32, (size, per), 1)
    dense = jnp.where((row % per) // bs == col // bs, tiled, 0)
    return dense.reshape(heads, per, per)


def kernel(x, norm_g, ffn_wg, ffn_wu, ffn_wd, ev_w_in, ev_w_out, lru_conv_w, lru_conv_b, lru_wa, lru_ba, lru_wx, lru_bx, lru_lambda, gla_w_gate, gla_b_gate, gla_norm, od_w_in, od_w_out, mlstm_conv_w, mlstm_conv_b, mlstm_wq, mlstm_wk, mlstm_wv, mlstm_w_if, mlstm_b_if, mlstm_skip, mlstm_norm, pool_w, pool_scale):
    bsz, seq, d = x.shape
    depth = norm_g.shape[0]
    d_half = lru_conv_w.shape[-1]
    t = bsz * seq
    row = lambda v: v.reshape(1, -1).astype(F32)
    wg16, wu16, wd16 = ffn_wg.astype(BF16), ffn_wu.astype(BF16), ffn_wd.astype(BF16)

    def ffn(xs, layer, which, g_pre, g_post):
        gains = jnp.stack([g_pre, 0.5 * g_post]).astype(F32)
        y = _ffn(xs.reshape(t, d), gains, wg16, wu16, wd16, layer, which)
        return y.reshape(bsz, seq, d)

    for layer in range(depth):
        g = norm_g[layer]
        j = layer // 2
        x = ffn(x, layer, 0, g[0], g[1])
        if layer % 2 == 0:
            w_in = _pad_cols(ev_w_in[j], MXU_DIM).astype(BF16)
            wab = jnp.concatenate([lru_wa[j], lru_wx[j]], axis=-1).astype(BF16)
            rank = gla_w_gate.shape[1]
            gup = jnp.pad(gla_w_gate[j], ((0, LANES - rank), (0, 0))).astype(BF16)
            x = _even_layer(
                x, row(g[2]), w_in, lru_conv_w[j], row(lru_conv_b[j]), wab, row(lru_ba[j]),
                row(lru_bx[j]), row(lru_lambda[j]), gup, row(gla_b_gate[j]), row(gla_norm[j]),
                ev_w_out[j].astype(BF16), row(g[3]))
        else:
            wq = _blockdiag_dense(mlstm_wq[j], MLSTM_HEADS).astype(BF16)
            wk = _blockdiag_dense(mlstm_wk[j], MLSTM_HEADS).astype(BF16)
            wv = _blockdiag_dense(mlstm_wv[j], MLSTM_HEADS).astype(BF16)
            wif = _pad_cols(mlstm_w_if[j], LANES).reshape(3, d_half, LANES).astype(BF16)
            bif = _pad_cols(mlstm_b_if[j].reshape(1, -1), LANES).astype(F32)
            x = _odd_layer(
                x, row(g[2]), od_w_in[j].astype(BF16), mlstm_conv_w[j], row(mlstm_conv_b[j]),
                wq, wk, wv, wif, bif, row(mlstm_skip[j]), row(mlstm_norm[j]),
                pool_w[j].astype(BF16), row(pool_scale[j]), od_w_out[j].astype(BF16), row(g[3]))
        x = ffn(x, layer, 1, g[4], g[5])
    return x
```

```python
import functools

import jax
import jax.numpy as jnp
from jax import lax
from jax.experimental import pallas as pl
from jax.experimental.pallas import tpu as pltpu

F32 = jnp.float32
BF16 = jnp.bfloat16

EPS = 1e-6
LANES = 128
SUBLANES = 8
MXU_DIM = 256
VMEM_LIMIT_BYTES = 56 * 1024 * 1024

CONV_W = 4
LRU_HEADS = 8
LRU_C = 8.0
GLA_HEADS = 4
GLA_TAU = 16.0
MLSTM_HEADS = 4
POOL_WINDOWS = (2, 4, 8, 16)
CHUNK = 64

FFN_ROW_TILE = 1024
FFN_HID_TILE = 512
FFN_EDGE_ROWS = 512
MIX_TIME_TILE = 256
PROJ_PIECE = MXU_DIM
CONV_HIST = SUBLANES
POOL_HIST = 16


def _rms(y, g):
    return y * lax.rsqrt(jnp.mean(y * y, axis=-1, keepdims=True) + EPS) * g


def _softplus(x):
    return jnp.maximum(x, 0.0) + jnp.log1p(jnp.exp(-jnp.abs(x)))


def _log_sigmoid(x):
    return -_softplus(-x)


def _dot(a, b):
    return jnp.dot(a, b, preferred_element_type=F32)


def _dot_nt(a, b):
    return lax.dot_general(a, b, (((1,), (1,)), ((), ())), preferred_element_type=F32)


def _dot_tn(a, b):
    return lax.dot_general(a, b, (((0,), (0,)), ((), ())), preferred_element_type=F32)


def _chunk_tril(n):
    row = lax.broadcasted_iota(jnp.int32, (n, n), 0)
    col = lax.broadcasted_iota(jnp.int32, (n, n), 1)
    same = (row // CHUNK) == (col // CHUNK)
    return jnp.where(same & (col <= row), 1.0, 0.0).astype(BF16)


def _chunk_cumsum(tril, x):
    x1 = x.astype(BF16)
    r1 = x - x1.astype(F32)
    x2 = r1.astype(BF16)
    x3 = (r1 - x2.astype(F32)).astype(BF16)
    return _dot(tril, x1) + _dot(tril, x2) + _dot(tril, x3)


def _proj_pieces(lhs, w_ref, p_refs, width):
    pieces = []
    c0 = 0
    for p_ref in p_refs:
        for l0 in range(0, p_ref.shape[1], width):
            def piece(p_ref=p_ref, l0=l0, c=c0 + l0, w=min(width, p_ref.shape[1] - l0)):
                p_ref[:, l0:l0 + w] = _dot(lhs(), w_ref[:, c:c + w])
            pieces.append(piece)
        c0 += p_ref.shape[1]
    assert c0 == w_ref.shape[1]
    return pieces


def _run(pieces):
    for piece in pieces:
        piece()


def _order_after(dst_ref, value, zero_ref):
    rows = SUBLANES * (4 // dst_ref.dtype.itemsize)
    bits = pltpu.bitcast(dst_ref[0:rows, 0:LANES], jnp.int32)
    dep = pltpu.bitcast(value[0:SUBLANES, 0:LANES], jnp.int32) & zero_ref[...]
    dst_ref[0:rows, 0:LANES] = pltpu.bitcast(bits | dep, dst_ref.dtype)


def _causal_conv(xs_ref, x, cw, cb, n):
    xs_ref[CONV_HIST:CONV_HIST + n, :] = x
    acc = cb
    for k in range(CONV_W):
        off = CONV_HIST - (CONV_W - 1) + k
        acc = acc + cw[k:k + 1, :] * xs_ref[off:off + n, :]
    xs_ref[0:CONV_HIST, :] = xs_ref[n:n + CONV_HIST, :]
    return acc


def _linear_scan(a, b, carry):
    n, c = a.shape
    groups = n // SUBLANES
    a3 = a.reshape(groups, SUBLANES, c)
    b3 = b.reshape(groups, SUBLANES, c)
    sub = lax.broadcasted_iota(jnp.int32, (groups, SUBLANES, c), 1)
    d = 1
    while d < SUBLANES:
        a_sh = jnp.where(sub >= d, pltpu.roll(a3, d, axis=1), 1.0)
        b_sh = jnp.where(sub >= d, pltpu.roll(b3, d, axis=1), 0.0)
        b3 = a3 * b_sh + b3
        a3 = a3 * a_sh
        d *= 2
    outs = []
    for v in range(groups):
        hv = b3[v] + a3[v] * carry
        outs.append(hv)
        carry = hv[SUBLANES - 1:SUBLANES, :]
    return jnp.concatenate(outs, axis=0), carry


def _ffn_step(x_ref, gpre_ref, wgu_ref, wd_ref, gpost_ref, o_ref, xn_ref, *,
              first, final, bounds):
    acc_ref = o_ref
    tf = wd_ref.shape[0]
    for r0, r1 in zip(bounds[:-1], bounds[1:]):
        rs = slice(r0, r1)
        if first:
            xn_ref[rs, :] = _rms(x_ref[rs, :], gpre_ref[...]).astype(BF16)
        xn = xn_ref[rs, :]
        ab = _dot(xn, wgu_ref[...])
        a, b = ab[:, 0:tf], ab[:, tf:2 * tf]
        h = (a * jax.nn.sigmoid(a) * b).astype(BF16)
        upd = _dot(h, wd_ref[...])
        acc = upd if first else acc_ref[rs, :] + upd
        if final:
            o_ref[rs, :] = x_ref[rs, :] + _rms(acc, gpost_ref[...])
        else:
            acc_ref[rs, :] = acc


def _ffn_body(*refs):
    j = pl.program_id(1)
    last = pl.num_programs(1) - 1
    tm = refs[0].shape[0]
    pl.when(j == 0)(functools.partial(_ffn_step, *refs, first=True, final=False,
                                      bounds=(0, FFN_EDGE_ROWS, tm)))
    pl.when(jnp.logical_and(j > 0, j < last))(
        functools.partial(_ffn_step, *refs, first=False, final=False, bounds=(0, tm)))
    pl.when(j == last)(functools.partial(_ffn_step, *refs, first=False, final=True,
                                         bounds=(0, tm - FFN_EDGE_ROWS, tm)))


def _ffn(x, g_pre, wgu, wd, g_post_half, layer, which):
    t, d = x.shape
    ff = wd.shape[-2]
    tm, tf = min(FFN_ROW_TILE, t), FFN_HID_TILE
    assert wgu.shape[2:] == (ff // tf, d, 2 * tf)
    assert t % tm == 0 and ff % tf == 0 and ff // tf >= 2 and tm > FFN_EDGE_ROWS
    return pl.pallas_call(
        _ffn_body,
        out_shape=jax.ShapeDtypeStruct((t, d), F32),
        grid=(t // tm, ff // tf),
        in_specs=[
            pl.BlockSpec((tm, d), lambda i, j: (i, 0)),
            pl.BlockSpec((1, d), lambda i, j: (0, 0)),
            pl.BlockSpec((None, None, None, d, 2 * tf), lambda i, j: (layer, which, j, 0, 0)),
            pl.BlockSpec((None, None, tf, d), lambda i, j: (layer, which, j, 0)),
            pl.BlockSpec((1, d), lambda i, j: (0, 0)),
        ],
        out_specs=pl.BlockSpec((tm, d), lambda i, j: (i, 0)),
        scratch_shapes=[pltpu.VMEM((tm, d), BF16)],
        compiler_params=pltpu.CompilerParams(
            dimension_semantics=("parallel", "arbitrary"),
            vmem_limit_bytes=VMEM_LIMIT_BYTES),
        name="ffn",
    )(x, g_pre, wgu, wd, g_post_half)


def _even_body(x_ref, g2_ref, win_ref, cw_ref, cb_ref, wab_ref, ba_ref, bx_ref, lam_ref, gup_ref,
               gb_ref, gn_ref, wout_ref, g3_ref, zero_ref, o_ref, px_ref, py_ref, pqk_ref, pv_ref,
               pr_ref, pg_ref, mixa_ref, mixb_ref, xs_ref, h_ref, st_ref, bc_ref, ma_ref, xn_ref,
               gates_ref, *, dh, dk, dv):
    n = o_ref.shape[0]
    d_half = LRU_HEADS * dh

    @pl.when(pl.program_id(1) == 0)
    def _():
        xs_ref[0:CONV_HIST, :] = jnp.zeros((CONV_HIST, d_half), F32)
        h_ref[...] = jnp.zeros_like(h_ref)
        st_ref[...] = jnp.zeros_like(st_ref)

    x = x_ref[...]
    xn_ref[...] = _rms(x, g2_ref[...]).astype(BF16)
    proj = _proj_pieces(lambda: xn_ref[...], win_ref,
                        (px_ref, py_ref, pqk_ref, pv_ref, pr_ref, pg_ref), PROJ_PIECE)
    per_group = d_half // PROJ_PIECE
    _run(proj[0:per_group])
    c_k = GLA_HEADS * dk

    xc = _causal_conv(xs_ref, px_ref[...], cw_ref[...], cb_ref[...], n)
    _run(proj[per_group:2 * per_group])
    xcb = xc.astype(BF16)
    sp = _softplus(-lam_ref[...])
    for h in range(LRU_HEADS):
        gates_ref[:, 2 * h * dh:2 * (h + 1) * dh] = _dot(xcb[:, h * dh:(h + 1) * dh], wab_ref[h])
    _order_after(xn_ref, gates_ref[:, 2 * (LRU_HEADS - 1) * dh:2 * LRU_HEADS * dh], zero_ref)
    rest = proj[-1:] + proj[2 * per_group:-1]
    share = -(-len(rest) // LRU_HEADS)
    for h in range(LRU_HEADS):
        _run(rest[h * share:(h + 1) * share])
        sl = slice(h * dh, (h + 1) * dh)
        gates = gates_ref[:, 2 * h * dh:2 * (h + 1) * dh]
        r = jax.nn.sigmoid(gates[:, 0:dh] + ba_ref[:, sl])
        i = jax.nn.sigmoid(gates[:, dh:2 * dh] + bx_ref[:, sl])
        log_a = -LRU_C * r * sp[:, sl]
        a = jnp.exp(log_a)
        b = jnp.sqrt(-jnp.tanh(log_a) * (a * a + 1.0)) * (i * xc[:, sl])
        hcol, h_ref[:, sl] = _linear_scan(a, b, h_ref[:, sl])
        mixa_ref[:, sl] = (hcol * jax.nn.gelu(py_ref[:, sl])).astype(mixa_ref.dtype)

    glr = pg_ref[:, 0:LANES].astype(BF16)
    gpre = _dot(glr, gup_ref[...]) + gb_ref[...]
    bc_ref[...] = _chunk_cumsum(_chunk_tril(n), _log_sigmoid(gpre) * (1.0 / GLA_TAU))
    crow =lax.broadcasted_iota(jnp.int32, (CHUNK, CHUNK), 0)
    ccol = lax.broadcasted_iota(jnp.int32, (CHUNK, CHUNK), 1)
    causal = ccol <= crow
    n_chunks = n // CHUNK
    ow = wout_ref.shape[1] // n_chunks
    parts = {}
    for c in range(n_chunks):
        ma_ref[:, c * ow:(c + 1) * ow] = _dot(mixa_ref[...], wout_ref[0:d_half, c * ow:(c + 1) * ow])
        rs = slice(c * CHUNK, (c + 1) * CHUNK)
        for h in range(GLA_HEADS):
            ks = slice(h * dk, (h + 1) * dk)
            bcum = bc_ref[rs, ks]
            q = pqk_ref[rs, ks]
            k = pqk_ref[rs, c_k + h * dk:c_k + (h + 1) * dk]
            v = pv_ref[rs, h * dv:(h + 1) * dv].astype(BF16)
            qd = ((q * (dk ** -0.5)) * jnp.exp(bcum)).astype(BF16)
            ki = (k * jnp.exp(-bcum)).astype(BF16)
            s = jnp.where(causal, _dot_nt(qd, ki), 0.0).astype(BF16)
            blast = bcum[CHUNK - 1:CHUNK, :]
            kd = (k * jnp.exp(blast - bcum)).astype(BF16)
            parts[c, h] = (qd, _dot(s, v), jnp.exp(blast), _dot_tn(v, kd))
    for c in range(n_chunks):
        rs = slice(c * CHUNK, (c + 1) * CHUNK)
        for h in range(GLA_HEADS):
            qd, o_intra, decay, kv = parts[c, h]
            st = st_ref[h]
            o = o_intra + _dot_nt(qd, st.astype(BF16))
            st_ref[h] = decay * st + kv
            rg = pr_ref[rs, h * dv:(h + 1) * dv]
            on = _rms(o, gn_ref[:, h * dv:(h + 1) * dv]) * (rg * jax.nn.sigmoid(rg))
            mixb_ref[rs, h * dv:(h + 1) * dv] = on.astype(mixb_ref.dtype)

    m = ma_ref[...] + _dot(mixb_ref[...], wout_ref[d_half:, :])
    o_ref[...] = x + _rms(m, g3_ref[...])


def _full(a):
    return pl.BlockSpec(a.shape, lambda b, t: (0,) * a.ndim)


def _even_layer(x, g2, w_in, cw, cb, wab, ba, bx, lam, gup, gb, gn, w_out, g3):
    bsz, s, d = x.shape
    d_half = cw.shape[1]
    n = min(MIX_TIME_TILE, s)
    assert s % n == 0 and n % CHUNK == 0 and w_in.shape[1] % MXU_DIM == 0
    dh = d_half // LRU_HEADS
    dk = gup.shape[1] // GLA_HEADS
    dv = d_half // GLA_HEADS
    zero = jnp.zeros((SUBLANES, LANES), jnp.int32)
    params = (g2, w_in, cw, cb, wab, ba, bx, lam, gup, gb, gn, w_out, g3, zero)
    tile = pl.BlockSpec((None, n, d), lambda b, t: (b, t, 0))
    return pl.pallas_call(
        functools.partial(_even_body, dh=dh, dk=dk, dv=dv),
        out_shape=jax.ShapeDtypeStruct((bsz, s, d), F32),
        grid=(bsz, s // n),
        in_specs=[tile] + [_full(a) for a in params],
        out_specs=tile,
        scratch_shapes=[
            pltpu.VMEM((n, d_half), F32),
            pltpu.VMEM((n, d_half), F32),
            pltpu.VMEM((n, 2 * GLA_HEADS * dk), F32),
            pltpu.VMEM((n, d_half), F32),
            pltpu.VMEM((n, d_half), F32),
            pltpu.VMEM((n, w_in.shape[1] - 4 * d_half - 2 * GLA_HEADS * dk), F32),
            pltpu.VMEM((n, d_half), BF16),
            pltpu.VMEM((n, d_half), BF16),
            pltpu.VMEM((n + CONV_HIST, d_half), F32),
            pltpu.VMEM((1, d_half), F32),
            pltpu.VMEM((GLA_HEADS, dv, dk), F32),
            pltpu.VMEM((n, GLA_HEADS * dk), F32),
            pltpu.VMEM((n, d), F32),
            pltpu.VMEM((n, d), BF16),
            pltpu.VMEM((n, 2 * d_half), F32),
        ],
        compiler_params=pltpu.CompilerParams(
            dimension_semantics=("parallel", "arbitrary"),
            vmem_limit_bytes=VMEM_LIMIT_BYTES),
        name="even_layer",
    )(x, *params)


def _odd_body(x_ref, g2_ref, win_ref, cw_ref, cb_ref, wq_ref, wk_ref, wv_ref, wif_ref, bif_ref,
              skip_ref, mn_ref, pw_ref, ps_ref, wout_ref, g3_ref, o_ref, pxm_ref, pz_ref, pu_ref,
              hn_ref, mixb_ref, mb_ref, xs_ref, us_ref, c_ref, n_ref, m_ref, xc_ref, q_ref, k_ref,
              v_ref, *, dh):
    n = o_ref.shape[0]
    d_half = MLSTM_HEADS * dh
    t = pl.program_id(1)

    @pl.when(t == 0)
    def _():
        xs_ref[0:CONV_HIST, :] = jnp.zeros((CONV_HIST, d_half), F32)
        us_ref[0:POOL_HIST, :] = jnp.zeros((POOL_HIST, d_half), F32)
        c_ref[...] = jnp.zeros_like(c_ref)
        n_ref[...] = jnp.zeros_like(n_ref)
        m_ref[...] = jnp.zeros_like(m_ref)

    x = x_ref[...]
    xn = _rms(x, g2_ref[...]).astype(BF16)
    proj = _proj_pieces(lambda: xn, win_ref, (pxm_ref, pz_ref, pu_ref), PROJ_PIECE)
    per_group = d_half // PROJ_PIECE
    _run(proj[0:per_group])

    xm = pxm_ref[...]
    xc = _causal_conv(xs_ref, xm, cw_ref[...], cb_ref[...], n)
    _run(proj[2 * per_group:3 * per_group])
    xc = xc * jax.nn.sigmoid(xc)
    xc_ref[...] = xc
    xcb = xc.astype(BF16)
    xmb = xm.astype(BF16)
    u = pu_ref[...]
    us_ref[POOL_HIST:POOL_HIST + n, :] = u
    pos = t * n + lax.broadcasted_iota(jnp.int32, (n, 1), 0)
    gw = d_half // len(POOL_WINDOWS)
    assert len(POOL_WINDOWS) == MLSTM_HEADS
    gates = bif_ref[...]
    for h in range(MLSTM_HEADS):
        cs = slice(h * gw, (h + 1) * gw)
        s = us_ref[:, cs]
        d = 1
        while d < POOL_WINDOWS[h]:
            s = s + pltpu.roll(s, d, axis=0)
            d *= 2
        cnt = jnp.minimum(pos + 1, POOL_WINDOWS[h]).astype(F32)
        pooled = s[POOL_HIST:POOL_HIST + n, :] / cnt - u[:, cs]
        y = _dot(pooled.astype(BF16), pw_ref[h]) * ps_ref[:, cs]
        mixb_ref[:, cs] = y.astype(mixb_ref.dtype)

        sl = slice(h * dh, (h + 1) * dh)
        q = _dot(xcb[:, sl], wq_ref[h])
        k = _dot(xcb[:, sl], wk_ref[h])
        v = _dot(xmb[:, sl], wv_ref[h])
        q_ref[:, sl] = q
        k_ref[:, sl] = k
        v_ref[:, sl] = v
        gates = (gates + _dot(q.astype(BF16), wif_ref[0, sl, :])
                 + _dot(k.astype(BF16), wif_ref[1, sl, :])
                 + _dot(v.astype(BF16), wif_ref[2, sl, :]))
    bcum_all = _chunk_cumsum(_chunk_tril(n), _log_sigmoid(gates))
    gates_t = gates.T
    bcum_t = bcum_all.T
    us_ref[0:POOL_HIST, :] = us_ref[n:n + POOL_HIST, :]

    crow = lax.broadcasted_iota(jnp.int32, (CHUNK, CHUNK), 0)
    ccol = lax.broadcasted_iota(jnp.int32, (CHUNK, CHUNK), 1)
    causal = ccol <= crow
    n_chunks = n // CHUNK

    def gate_views(c, h):
        rs = slice(c * CHUNK, (c + 1) * CHUNK)
        fcol = MLSTM_HEADS + h
        b_c = bcum_all[rs, fcol:fcol + 1]
        b_r = bcum_t[fcol:fcol + 1, rs]
        i_c = gates[rs, h:h + 1]
        i_r = gates_t[h:h + 1, rs]
        return b_c, b_r, i_c, i_r

    m_at = {}
    for h in range(MLSTM_HEADS):
        m_run = m_ref[h]
        for c in range(n_chunks):
            b_c, b_r, _, i_r = gate_views(c, h)
            b_l = b_c[CHUNK - 1:CHUNK, :]
            m_at[c, h] = m_run
            m_run = jnp.maximum(b_l + m_run, jnp.max(b_l - b_r + i_r, axis=-1, keepdims=True))
        m_at[n_chunks, h] = m_run
        m_ref[h] = m_run

    zpieces = proj[per_group:2 * per_group]
    zshare = -(-len(zpieces) // n_chunks)
    ow = wout_ref.shape[1] // n_chunks
    parts = {}
    for c in range(n_chunks):
        _run(zpieces[c * zshare:(c + 1) * zshare])
        mb_ref[:, c * ow:(c + 1) * ow] = _dot(mixb_ref[...], wout_ref[d_half:, c * ow:(c + 1) * ow])
        rs = slice(c * CHUNK, (c + 1) * CHUNK)
        for h in range(MLSTM_HEADS):
            sl = slice(h * dh, (h + 1) * dh)
            b_c, b_r, i_c, i_r = gate_views(c, h)
            m_prev, m_new = m_at[c, h], m_at[c + 1, h]
            dmat = jnp.where(causal, b_c - b_r + i_r, -jnp.inf)
            inter = b_c + m_prev
            m_t = jnp.maximum(inter, jnp.max(dmat, axis=-1, keepdims=True))
            w = jnp.exp(dmat - m_t)
            inter_w = jnp.exp(inter - m_t)
            qb = q_ref[rs, sl].astype(BF16)
            k = k_ref[rs, sl] * (dh ** -0.5)
            kb = k.astype(BF16)
            v = v_ref[rs, sl]
            qk = _dot_nt(qb, kb) * w
            b_l = b_c[CHUNK - 1:CHUNK, :]
            decay = jnp.exp(b_l + m_prev - m_new)
            w_c = jnp.exp(b_l - b_c + i_c - m_new)
            parts[c, h] = (
                qb, _dot(qk.astype(BF16), v.astype(BF16)), jnp.sum(qk, axis=-1, keepdims=True),
                inter_w, jnp.exp(-m_t), decay, _dot_tn((v * w_c).astype(BF16), kb),
                jnp.sum(k * w_c, axis=0, keepdims=True))
    for c in range(n_chunks):
        rs = slice(c * CHUNK, (c + 1) * CHUNK)
        for h in range(MLSTM_HEADS):
            sl = slice(h * dh, (h + 1) * dh)
            qb, num_intra, den_intra, inter_w, floor, decay, kvw, ksum = parts[c, h]
            cst = c_ref[h]
            nst = n_ref[h]
            num = num_intra + inter_w * _dot_nt(qb, cst.astype(BF16))
            den = den_intra + inter_w * jnp.sum(q_ref[rs, sl] * nst, axis=-1, keepdims=True)
            hid = num / jnp.maximum(jnp.abs(den), floor)
            c_ref[h] = decay * cst + kvw
            n_ref[h] = decay * nst + ksum
            hn_ref[rs, sl] = _rms(hid, mn_ref[:, sl])

    m = mb_ref[...]
    for h in range(MLSTM_HEADS):
        sl = slice(h * dh, (h + 1) * dh)
        out = jax.nn.sigmoid(pz_ref[:, sl]) * (hn_ref[:, sl] + skip_ref[:, sl] * xc_ref[:, sl])
        m = m + _dot(out.astype(BF16), wout_ref[sl, :])
    o_ref[...] = x + _rms(m, g3_ref[...])


def _odd_layer(x, g2, w_in, cw, cb, wq, wk, wv, wif, bif, skip, mn, pw, ps, w_out, g3):
    bsz, s, d = x.shape
    d_half = cw.shape[1]
    n = min(MIX_TIME_TILE, s)
    assert s % n == 0 and n % CHUNK == 0 and w_in.shape[1] == 3 * d_half
    dh = d_half // MLSTM_HEADS
    params = (g2, w_in, cw, cb, wq, wk, wv, wif, bif, skip, mn, pw, ps, w_out, g3)
    tile = pl.BlockSpec((None, n, d), lambda b, t: (b, t, 0))
    return pl.pallas_call(
        functools.partial(_odd_body, dh=dh),
        out_shape=jax.ShapeDtypeStruct((bsz, s, d), F32),
        grid=(bsz, s // n),
        in_specs=[tile] + [_full(a) for a in params],
        out_specs=tile,
        scratch_shapes=[
            pltpu.VMEM((n, d_half), F32),
            pltpu.VMEM((n, d_half), F32),
            pltpu.VMEM((n, d_half), F32),
            pltpu.VMEM((n, d_half), F32),
            pltpu.VMEM((n, d_half), BF16),
            pltpu.VMEM((n, d), F32),
            pltpu.VMEM((n + CONV_HIST, d_half), F32),
            pltpu.VMEM((n + POOL_HIST, d_half), F32),
            pltpu.VMEM((MLSTM_HEADS, dh, dh), F32),
            pltpu.VMEM((MLSTM_HEADS, 1, dh), F32),
            pltpu.VMEM((MLSTM_HEADS, 1, 1), F32),
            pltpu.VMEM((n, d_half), F32),
            pltpu.VMEM((n, d_half), F32),
            pltpu.VMEM((n, d_half), F32),
            pltpu.VMEM((n, d_half), F32),
        ],
        compiler_params=pltpu.CompilerParams(
            dimension_semantics=("parallel", "arbitrary"),
            vmem_limit_bytes=VMEM_LIMIT_BYTES),
        name="odd_layer",
    )(x, *params)


def _pad_cols(w, mult):
    pad = (-w.shape[-1]) % mult
    return jnp.pad(w, [(0, 0)] * (w.ndim - 1) + [(0, pad)])


def _blockdiag_dense(w, heads):
    nb, bs, _ = w.shape
    size = nb * bs
    per = size // heads
    tiled = jnp.tile(w.reshape(size, bs), (1, per // bs))
    row = lax.broadcasted_iota(jnp.int32, (size, per), 0)
    col = lax.broadcasted_iota(jnp.int32, (size, per), 1)
    dense = jnp.where((row % per) // bs == col // bs, tiled, 0)
    return dense.reshape(heads, per, per)


def kernel(x, norm_g, ffn_wg, ffn_wu, ffn_wd, ev_w_in, ev_w_out, lru_conv_w, lru_conv_b, lru_wa, lru_ba, lru_wx, lru_bx, lru_lambda, gla_w_gate, gla_b_gate, gla_norm, od_w_in, od_w_out, mlstm_conv_w, mlstm_conv_b, mlstm_wq, mlstm_wk, mlstm_wv, mlstm_w_if, mlstm_b_if, mlstm_skip, mlstm_norm, pool_w, pool_scale):
    bsz, seq, d = x.shape
    depth = norm_g.shape[0]
    d_half = lru_conv_w.shape[-1]
    t = bsz * seq
    row = lambda v: v.reshape(1, -1).astype(F32)
    tiles = (depth, 2, d, ffn_wg.shape[-1] // FFN_HID_TILE, FFN_HID_TILE)
    wgu16 = jnp.concatenate([ffn_wg.reshape(tiles), ffn_wu.reshape(tiles)], axis=-1)
    wgu16 = wgu16.transpose(0, 1, 3, 2, 4).astype(BF16)
    wd16 = ffn_wd.astype(BF16)

    def ffn(xs, layer, which, g_pre, g_post):
        y = _ffn(xs.reshape(t, d), row(g_pre), wgu16, wd16, row(0.5 * g_post), layer, which)
        return y.reshape(bsz, seq, d)

    for layer in range(depth):
        g = norm_g[layer]
        j = layer // 2
        x = ffn(x, layer, 0, g[0], g[1])
        if layer % 2 == 0:
            w_in = _pad_cols(ev_w_in[j], MXU_DIM).astype(BF16)
            wab = jnp.concatenate([lru_wa[j], lru_wx[j]], axis=-1).astype(BF16)
            rank = gla_w_gate.shape[1]
            gup = jnp.pad(gla_w_gate[j], ((0, LANES - rank), (0, 0))).astype(BF16)
            x = _even_layer(
                x, row(g[2]), w_in, lru_conv_w[j], row(lru_conv_b[j]), wab, row(lru_ba[j]),
                row(lru_bx[j]), row(lru_lambda[j]), gup, row(gla_b_gate[j]), row(gla_norm[j]),
                ev_w_out[j].astype(BF16), row(g[3]))
        else:
            wq = _blockdiag_dense(mlstm_wq[j], MLSTM_HEADS).astype(BF16)
            wk = _blockdiag_dense(mlstm_wk[j], MLSTM_HEADS).astype(BF16)
            wv = _blockdiag_dense(mlstm_wv[j], MLSTM_HEADS).astype(BF16)
            wif = _pad_cols(mlstm_w_if[j], LANES).reshape(3, d_half, LANES).astype(BF16)
            bif = _pad_cols(mlstm_b_if[j].reshape(1, -1), LANES).astype(F32)
            x = _odd_layer(
                x, row(g[2]), od_w_in[j].astype(BF16), mlstm_conv_w[j], row(mlstm_conv_b[j]),
                wq, wk, wv, wif, bif, row(mlstm_skip[j]), row(mlstm_norm[j]),
                pool_w[j].astype(BF16), row(pool_scale[j]), od_w_out[j].astype(BF16), row(g[3]))
        x = ffn(x, layer, 1, g[4], g[5])
    return x
```
